```python
import jax, jax.numpy as jnp
from jax import lax
import numpy as np

D_MODEL = 1024
BATCH = 4
SEQ = 4096
DEPTH = 4
DEC_BATCH = 32
DEC_SEQ = 4
PAST_LEN = 8192
PAGE_SIZE = 128

N_EVEN = (DEPTH + 1) // 2
N_ODD = DEPTH // 2
W_M = D_MODEL // 2
H_M = 4
DH_M = W_M // H_M
CONV_W = 4
W_F = D_MODEL // 2
H_F = 8
DH_F = W_F // H_F
FORGET_BIAS = 3.0
W_D = D_MODEL
H_D = 8
DV_D = W_D // H_D
DQ_D = DV_D // 2
Q_BLOCK = 128
M_CHUNK = 128
EPS = 1e-6
SPLIT_EVEN = (W_M, W_M, W_M, W_M, W_M, H_M, H_M, W_F, W_F, W_F, W_F, H_F)
P_EVEN = sum(SPLIT_EVEN)
P_ODD = 4 * W_D

kernel_name = "hybrid_mlstm_fox_diffattn_step"


def rmsnorm(x, g):
    xf = x.astype(jnp.float32)
    y = xf * lax.rsqrt(jnp.mean(xf * xf, axis=-1, keepdims=True) + EPS)
    return (y * g.astype(jnp.float32)).astype(x.dtype)


def split_cols(p, sizes):
    idx = np.cumsum(sizes)[:-1].tolist()
    return jnp.split(p, idx, axis=-1)


def causal_conv(u, buf, w, b):
    T = u.shape[1]
    ext = jnp.concatenate([buf, u], axis=1)
    y = sum(ext[:, j:j + T] * w[j] for j in range(CONV_W)) + b
    return y, ext[:, T:]


def mlstm_chunk(carry, xs):
    C, n, m = carry
    q, k, v, ig, lf = xs
    q, k, v = (a.astype(jnp.float32) for a in (q, k, v))
    L = q.shape[1]
    F = jnp.cumsum(lf.astype(jnp.float32), axis=1).transpose(0, 2, 1)
    ig = ig.astype(jnp.float32).transpose(0, 2, 1)
    causal = jnp.tril(jnp.ones((L, L), dtype=bool))
    D = jnp.where(causal, F[..., :, None] - F[..., None, :] + ig[..., None, :], -jnp.inf)
    inter = F + m[..., None]
    m_t = jnp.maximum(jnp.max(D, axis=-1), inter)
    S = jnp.exp(D - m_t[..., None]) * jnp.einsum('bthd,bshd->bhts', q, k)
    w_inter = jnp.exp(inter - m_t)
    num = jnp.einsum('bhts,bshd->bhtd', S, v) + w_inter[..., None] * jnp.einsum('bthd,bhde->bhte', q, C)
    den = jnp.sum(S, axis=-1) + w_inter * jnp.einsum('bthd,bhd->bht', q, n)
    h = num / jnp.maximum(jnp.abs(den), jnp.exp(-m_t))[..., None]
    m_new = m_t[..., -1]
    w_s = jnp.exp(F[..., -1:] - F + ig - m_new[..., None])
    decay = jnp.exp(F[..., -1] + m - m_new)
    C_new = decay[..., None, None] * C + jnp.einsum('bhs,bshd,bshe->bhde', w_s, k, v)
    n_new = decay[..., None] * n + jnp.einsum('bhs,bshd->bhd', w_s, k)
    return (C_new, n_new, m_new), h.transpose(0, 2, 1, 3)


def masked_probs(q, k, qpos, kpos, scale, fq=None, fk=None):
    s = jnp.einsum('bqhd,bkhd->bhqk', q, k).astype(jnp.float32) * scale
    if fq is not None:
        s = s + fq.transpose(0, 2, 1)[..., :, None] - fk.transpose(0, 2, 1)[..., None, :]
    s = jnp.where(kpos[None, :] <= qpos[:, None], s, -jnp.inf)
    return jax.nn.softmax(s, axis=-1)


def fox_attend(q, fq, qpos, k, v, fk, kpos):
    p = masked_probs(q, k, qpos, kpos, DH_F ** -0.5, fq, fk)
    return jnp.einsum('bhqk,bkhd->bqhd', p.astype(v.dtype), v)


def diff_attend(q, qpos, k, v, kpos, lam):
    s = DQ_D ** -0.5
    p1 = masked_probs(q[..., 0, :], k[..., 0, :], qpos, kpos, s)
    p2 = masked_probs(q[..., 1, :], k[..., 1, :], qpos, kpos, s)
    return jnp.einsum('bhqk,bkhd->bqhd', (p1 - lam * p2).astype(v.dtype), v)


def sweep_blocks(fn, T, *q_side):
    nb = T // Q_BLOCK
    def to_blocks(a):
        return a.reshape(a.shape[0], nb, Q_BLOCK, *a.shape[2:]).swapaxes(0, 1)
    qpos = jnp.arange(T, dtype=jnp.int32).reshape(nb, Q_BLOCK)
    out = lax.map(lambda args: fn(*args), (qpos, *[to_blocks(a) for a in q_side]))
    o = out.swapaxes(0, 1)
    return o.reshape(o.shape[0], T, *o.shape[3:])


def gather_pages(pool, page_table):
    g = pool[page_table]
    return g.reshape(g.shape[0], -1, *g.shape[3:])


def even_layer(x, norm_g, w_in, b_in, conv_w, conv_b, mnorm_g, w_out, conv_buf, m_carry, fox_past):
    B, T, _ = x.shape
    p = rmsnorm(x, norm_g) @ w_in + b_in
    q_m, k_m, v_m, o_m, z_m, i_m, f_m, q_f, k_f, v_f, z_f, f_f = split_cols(p, SPLIT_EVEN)
    qk, new_buf = causal_conv(jnp.concatenate([q_m, k_m], axis=-1), conv_buf, conv_w, conv_b)
    qk = jax.nn.silu(qk)
    qm = qk[..., :W_M].reshape(B, T, H_M, DH_M)
    km = qk[..., W_M:].reshape(B, T, H_M, DH_M) * (DH_M ** -0.5)
    vm = v_m.reshape(B, T, H_M, DH_M)
    lfm = jax.nn.log_sigmoid(f_m.astype(jnp.float32))
    if fox_past is None:
        nc = T // M_CHUNK
        def to_chunks(a):
            return a.reshape(B, nc, M_CHUNK, *a.shape[2:]).swapaxes(0, 1)
        m_carry, hc = lax.scan(mlstm_chunk, m_carry, tuple(to_chunks(a) for a in (qm, km, vm, i_m, lfm)))
        hm = hc.swapaxes(0, 1).reshape(B, T, H_M, DH_M)
    else:
        m_carry, hm = mlstm_chunk(m_carry, (qm, km, vm, i_m, lfm))
    hm = rmsnorm(hm, mnorm_g.reshape(H_M, DH_M)).reshape(B, T, W_M).astype(x.dtype)
    hm = hm * jax.nn.sigmoid(o_m) * jax.nn.silu(z_m)
    qf = q_f.reshape(B, T, H_F, DH_F)
    kf = k_f.reshape(B, T, H_F, DH_F)
    vf = v_f.reshape(B, T, H_F, DH_F)
    lff = jax.nn.log_sigmoid(f_f.astype(jnp.float32))
    if fox_past is None:
        F = jnp.cumsum(lff, axis=1)
        kpos = jnp.arange(T, dtype=jnp.int32)
        hf = sweep_blocks(lambda qpos, qb, fb: fox_attend(qb, fb, qpos, kf, vf, F, kpos), T, qf, F)
    else:
        k_past, v_past, lf_past = fox_past
        P = k_past.shape[1]
        k_all = jnp.concatenate([k_past.astype(kf.dtype), kf], axis=1)
        v_all = jnp.concatenate([v_past.astype(vf.dtype), vf], axis=1)
        F = jnp.cumsum(jnp.concatenate([lf_past.astype(jnp.float32), lff], axis=1), axis=1)
        kpos = jnp.arange(P + T, dtype=jnp.int32)
        qpos = P + jnp.arange(T, dtype=jnp.int32)
        hf = fox_attend(qf, F[:, P:], qpos, k_all, v_all, F, kpos)
    hf = hf.reshape(B, T, W_F) * jax.nn.silu(z_f)
    y = x + jnp.concatenate([hm, hf], axis=-1) @ w_out
    C, n, m = m_carry
    return y, (kf, vf, lff.astype(x.dtype), C, n, m, new_buf)


def odd_layer(x, norm_g, w_in, lq1, lk1, lq2, lk2, dnorm_g, w_out, lam_init, past):
    B, T, _ = x.shape
    q, k, v, z = split_cols(rmsnorm(x, norm_g) @ w_in, (W_D, W_D, W_D, W_D))
    q = q.reshape(B, T, H_D, 2, DQ_D)
    k = k.reshape(B, T, H_D, 2, DQ_D)
    v = v.reshape(B, T, H_D, DV_D)
    lam = (jnp.exp(jnp.sum(lq1.astype(jnp.float32) * lk1.astype(jnp.float32)))
           - jnp.exp(jnp.sum(lq2.astype(jnp.float32) * lk2.astype(jnp.float32))) + lam_init)
    if past is None:
        kpos = jnp.arange(T, dtype=jnp.int32)
        o = sweep_blocks(lambda qpos, qb: diff_attend(qb, qpos, k, v, kpos, lam), T, q)
    else:
        k_past, v_past = past
        P = k_past.shape[1]
        k_all = jnp.concatenate([k_past.reshape(B, P, H_D, 2, DQ_D).astype(k.dtype), k], axis=1)
        v_all = jnp.concatenate([v_past.astype(v.dtype), v], axis=1)
        kpos = jnp.arange(P + T, dtype=jnp.int32)
        qpos = P + jnp.arange(T, dtype=jnp.int32)
        o = diff_attend(q, qpos, k_all, v_all, kpos, lam)
    o = (rmsnorm(o, dnorm_g) * (1.0 - lam_init)).reshape(B, T, W_D) * jax.nn.silu(z)
    y = x + o @ w_out
    return y, (k.reshape(B, T, H_D, 2 * DQ_D), v)


def setup_inputs(seed: int = 0) -> dict:
    key = jax.random.key(seed)
    ks = jax.random.split(key, 32)
    f32 = jnp.float32
    def nrm(k, shape, s=1.0):
        return s * jax.random.normal(k, shape, f32)
    n_pages = PAST_LEN // PAGE_SIZE
    n_pool = (DEC_BATCH * n_pages * 5) // 4
    page_table = jax.random.permutation(ks[11], n_pool)[:DEC_BATCH * n_pages].reshape(DEC_BATCH, n_pages).astype(jnp.int32)
    gate_off = np.zeros((P_EVEN,), np.float32)
    f_m0 = 5 * W_M + H_M
    gate_off[f_m0:f_m0 + H_M] = FORGET_BIAS
    gate_off[P_EVEN - H_F:] = FORGET_BIAS
    return {
        "x_prompt": nrm(ks[0], (BATCH, SEQ, D_MODEL)),
        "x_sample": nrm(ks[1], (DEC_BATCH, DEC_SEQ, D_MODEL)),
        "cache_fox_k": nrm(ks[2], (N_EVEN, n_pool, PAGE_SIZE, H_F, DH_F)),
        "cache_fox_v": nrm(ks[3], (N_EVEN, n_pool, PAGE_SIZE, H_F, DH_F)),
        "cache_fox_logf": jax.nn.log_sigmoid(FORGET_BIAS + nrm(ks[4], (N_EVEN, n_pool, PAGE_SIZE, H_F))),
        "cache_diff_k": nrm(ks[5], (N_ODD, n_pool, PAGE_SIZE, H_D, 2 * DQ_D)),
        "cache_diff_v": nrm(ks[6], (N_ODD, n_pool, PAGE_SIZE, H_D, DV_D)),
        "state_mlstm_c": nrm(ks[7], (N_EVEN, DEC_BATCH, H_M, DH_M, DH_M), 0.2),
        "state_mlstm_n": nrm(ks[8], (N_EVEN, DEC_BATCH, H_M, DH_M), 0.5),
        "state_mlstm_m": 2.0 + nrm(ks[9], (N_EVEN, DEC_BATCH, H_M)),
        "state_mlstm_conv": nrm(ks[10], (N_EVEN, DEC_BATCH, CONV_W - 1, 2 * W_M)),
        "page_table": page_table,
        "norm_g": 1.0 + nrm(ks[12], (DEPTH, D_MODEL), 0.02),
        "final_norm_g": 1.0 + nrm(ks[13], (D_MODEL,), 0.02),
        "w_in_even": nrm(ks[14], (N_EVEN, D_MODEL, P_EVEN), D_MODEL ** -0.5),
        "b_in_even": nrm(ks[15], (N_EVEN, P_EVEN), 0.02) + jnp.asarray(gate_off),
        "conv_w": nrm(ks[16], (N_EVEN, CONV_W, 2 * W_M), CONV_W ** -0.5),
        "conv_b": nrm(ks[17], (N_EVEN, 2 * W_M), 0.02),
        "mlstm_norm_g": 1.0 + nrm(ks[18], (N_EVEN, W_M), 0.02),
        "w_out_even": nrm(ks[19], (N_EVEN, W_M + W_F, D_MODEL), (W_M + W_F) ** -0.5),
        "w_in_odd": nrm(ks[20], (N_ODD, D_MODEL, P_ODD), D_MODEL ** -0.5),
        "lambda_q1": nrm(ks[21], (N_ODD, DQ_D), 0.1),
        "lambda_k1": nrm(ks[22], (N_ODD, DQ_D), 0.1),
        "lambda_q2": nrm(ks[23], (N_ODD, DQ_D), 0.1),
        "lambda_k2": nrm(ks[24], (N_ODD, DQ_D), 0.1),
        "diff_norm_g": 1.0 + nrm(ks[25], (N_ODD, DV_D), 0.02),
        "w_out_odd": nrm(ks[26], (N_ODD, W_D, D_MODEL), W_D ** -0.5),
    }


def reference(x_prompt, x_sample, cache_fox_k, cache_fox_v, cache_fox_logf, cache_diff_k, cache_diff_v,
              state_mlstm_c, state_mlstm_n, state_mlstm_m, state_mlstm_conv, page_table,
              norm_g, final_norm_g, w_in_even, b_in_even, conv_w, conv_b, mlstm_norm_g, w_out_even,
              w_in_odd, lambda_q1, lambda_k1, lambda_q2, lambda_k2, diff_norm_g, w_out_odd):
    hp, hs = x_prompt, x_sample
    Bp, Bs = x_prompt.shape[0], x_sample.shape[0]
    names = ("fox_k", "fox_v", "fox_logf", "diff_k", "diff_v", "mlstm_c", "mlstm_n", "mlstm_m", "mlstm_conv")
    P = {nm: [] for nm in names}
    S = {nm: [] for nm in names}
    even_names = ("fox_k", "fox_v", "fox_logf", "mlstm_c", "mlstm_n", "mlstm_m", "mlstm_conv")
    for l in range(DEPTH):
        i = l // 2
        if l % 2 == 0:
            ew = (norm_g[l], w_in_even[i], b_in_even[i], conv_w[i], conv_b[i], mlstm_norm_g[i], w_out_even[i])
            carry0 = (jnp.zeros((Bp, H_M, DH_M, DH_M), jnp.float32), jnp.zeros((Bp, H_M, DH_M), jnp.float32),
                      jnp.zeros((Bp, H_M), jnp.float32))
            buf0 = jnp.zeros((Bp, CONV_W - 1, 2 * W_M), hp.dtype)
            hp, stp = even_layer(hp, *ew, buf0, carry0, None)
            carry_s = (state_mlstm_c[i].astype(jnp.float32), state_mlstm_n[i].astype(jnp.float32),
                       state_mlstm_m[i].astype(jnp.float32))
            past = (gather_pages(cache_fox_k[i], page_table), gather_pages(cache_fox_v[i], page_table),
                    gather_pages(cache_fox_logf[i], page_table))
            hs, sts = even_layer(hs, *ew, state_mlstm_conv[i].astype(hs.dtype), carry_s, past)
            for nm, a, b in zip(even_names, stp, sts):
                P[nm].append(a)
                S[nm].append(b)
        else:
            lam_init = 0.8 - 0.6 * float(np.exp(-0.3 * l))
            ow = (norm_g[l], w_in_odd[i], lambda_q1[i], lambda_k1[i], lambda_q2[i], lambda_k2[i], diff_norm_g[i], w_out_odd[i])
            hp, (kp, vp) = odd_layer(hp, *ow, lam_init, None)
            past = (gather_pages(cache_diff_k[i], page_table), gather_pages(cache_diff_v[i], page_table))
            hs, (ks_, vs_) = odd_layer(hs, *ow, lam_init, past)
            P["diff_k"].append(kp)
            P["diff_v"].append(vp)
            S["diff_k"].append(ks_)
            S["diff_v"].append(vs_)
    y_prompt = rmsnorm(hp, final_norm_g)
    y_sample = rmsnorm(hs, final_norm_g)
    return (y_prompt, y_sample,
            jnp.stack(P["fox_k"]), jnp.stack(P["fox_v"]), jnp.stack(P["fox_logf"]),
            jnp.stack(P["diff_k"]), jnp.stack(P["diff_v"]),
            jnp.stack(P["mlstm_c"]), jnp.stack(P["mlstm_n"]), jnp.stack(P["mlstm_m"]), jnp.stack(P["mlstm_conv"]),
            jnp.stack(S["fox_k"]), jnp.stack(S["fox_v"]), jnp.stack(S["fox_logf"]),
            jnp.stack(S["diff_k"]), jnp.stack(S["diff_v"]),
            jnp.stack(S["mlstm_c"]), jnp.stack(S["mlstm_n"]), jnp.stack(S["mlstm_m"]), jnp.stack(S["mlstm_conv"]))
```

```python
import functools

import numpy as np
import jax
import jax.numpy as jnp
from jax import lax
from jax.experimental import pallas as pl
from jax.experimental.pallas import tpu as pltpu

F32 = jnp.float32
BF16 = jnp.bfloat16
NEG_INF = float("-inf")

EPS = 1e-6
H_M, DH_M = 4, 128
H_F, DH_F = 8, 64
H_D, DV_D, DQ_D = 8, 128, 64
CONV_W = 4
M_CHUNK = 128
LANES = 128
SUBLANES = 8
VMEM_LIMIT = 52 * 1024 * 1024

W_M = H_M * DH_M
W_F = H_F * DH_F
W_D = H_D * DV_D


def _cparams(sem):
    return pltpu.CompilerParams(dimension_semantics=sem, vmem_limit_bytes=VMEM_LIMIT)


def _sigmoid(x):
    return 1.0 / (1.0 + jnp.exp(-x))


def _silu(x):
    return x * _sigmoid(x)


def _log_sigmoid(x):
    return jnp.minimum(x, 0.0) - jnp.log(1.0 + jnp.exp(-jnp.abs(x)))


def _dot_nt(a, b):
    return lax.dot_general(a, b, (((1,), (1,)), ((), ())), preferred_element_type=F32)


def _proj_body(plan, has_bias, x_ref, g_ref, w_ref, *rest):
    if has_bias:
        b_ref, *out_refs = rest
    else:
        out_refs = rest
    x = x_ref[...]
    xn = x * lax.rsqrt(jnp.mean(x * x, axis=-1, keepdims=True) + EPS) * g_ref[...]
    xn = xn.astype(BF16)
    k = 0
    for c0, width, dtypes in plan:
        y = jnp.dot(xn, w_ref[:, c0:c0 + width], preferred_element_type=F32)
        if has_bias:
            y = y + b_ref[:, c0:c0 + width]
        for dt in dtypes:
            out_refs[k][...] = y.astype(dt)
            k += 1


def _proj(x, g, w, b, plan, tm):
    M, D = x.shape
    N = w.shape[1]
    has_bias = b is not None
    out_shape, out_specs = [], []
    for _, width, dtypes in plan:
        for dt in dtypes:
            out_shape.append(jax.ShapeDtypeStruct((M, width), dt))
            out_specs.append(pl.BlockSpec((tm, width), lambda i: (i, 0)))
    in_specs = [pl.BlockSpec((tm, D), lambda i: (i, 0)),
                pl.BlockSpec((1, D), lambda i: (0, 0)),
                pl.BlockSpec((D, N), lambda i: (0, 0))]
    args = [x, g.reshape(1, D), w]
    if has_bias:
        in_specs.append(pl.BlockSpec((1, N), lambda i: (0, 0)))
        args.append(b.reshape(1, N))
    return pl.pallas_call(
        functools.partial(_proj_body, plan, has_bias),
        grid=(M // tm,), in_specs=in_specs, out_specs=out_specs, out_shape=out_shape,
        compiler_params=_cparams(("arbitrary",)), name="proj",
    )(*args)


def _mlstm_body(Lv, has_state, qkv_ref, oz_ref, gt_ref, cw_ref, cb_ref, mg_ref, *rest):
    L = M_CHUNK
    if has_state:
        c0_ref, n0_ref, m0_ref, cv0_ref, *rest = rest
    h_ref, cN_ref, nN_ref, mN_ref, cvN_ref, ubuf, *pads = rest
    c = pl.program_id(1)

    @pl.when(c == 0)
    def _():
        ubuf[...] = jnp.zeros(ubuf.shape, F32)
        if has_state:
            ubuf[SUBLANES - (CONV_W - 1):SUBLANES, :] = cv0_ref[...]
            cN_ref[...] = c0_ref[...]
            nN_ref[...] = n0_ref[...]
            mN_ref[...] = m0_ref[...]
        else:
            cN_ref[...] = jnp.zeros(cN_ref.shape, F32)
            nN_ref[...] = jnp.zeros(nN_ref.shape, F32)
            mN_ref[...] = jnp.zeros(mN_ref.shape, F32)

    ubuf[SUBLANES:SUBLANES + Lv, :] = qkv_ref[:, 0:2 * W_M]
    acc = None
    for j in range(CONV_W):
        r0 = SUBLANES - (CONV_W - 1) + j
        term = ubuf[r0:r0 + L, :] * cw_ref[j:j + 1, :]
        acc = term if acc is None else acc + term
    y = acc + cb_ref[...]
    tail = ubuf[SUBLANES + Lv - (CONV_W - 1):SUBLANES + Lv, :]
    ubuf[SUBLANES - (CONV_W - 1):SUBLANES, :] = tail
    cvN_ref[...] = tail
    qk = _silu(y)

    if Lv == L:
        G = gt_ref[...]
        V = qkv_ref[:, 2 * W_M:3 * W_M]
    else:
        gbuf, vbuf = pads
        gbuf[...] = jnp.zeros(gbuf.shape, F32)
        gbuf[0:Lv, :] = gt_ref[...]
        vbuf[...] = jnp.zeros(vbuf.shape, F32)
        vbuf[0:Lv, :] = qkv_ref[:, 2 * W_M:3 * W_M]
        G = gbuf[...]
        V = vbuf[...]
    LF = _log_sigmoid(G)
    GT = G.T
    LFT = LF.T
    r_i = lax.broadcasted_iota(jnp.int32, (L, L), 0)
    c_i = lax.broadcasted_iota(jnp.int32, (L, L), 1)
    tril = r_i >= c_i
    triu = r_i <= c_i
    rowv = lax.broadcasted_iota(jnp.int32, (L, 1), 0) < Lv
    colv = lax.broadcasted_iota(jnp.int32, (1, L), 1) < Lv

    for h in range(H_M):
        hs = slice(h * DH_M, (h + 1) * DH_M)
        q = qk[:, h * DH_M:(h + 1) * DH_M]
        k = qk[:, W_M + h * DH_M:W_M + (h + 1) * DH_M] * (DH_M ** -0.5)
        vb = V[:, hs].astype(BF16)
        ig_row = GT[h:h + 1, :]
        lf_row = LFT[H_M + h:H_M + h + 1, :]
        ig_col = G[:, h:h + 1]
        lf_col = LF[:, H_M + h:H_M + h + 1]
        if Lv < L:
            lf_row = jnp.where(colv, lf_row, 0.0)
            lf_col = jnp.where(rowv, lf_col, 0.0)
            ig_row = jnp.where(colv, ig_row, NEG_INF)
            ig_col = jnp.where(rowv, ig_col, NEG_INF)
        F_col = jnp.sum(jnp.where(tril, lf_row, 0.0), axis=1, keepdims=True)
        F_row = jnp.sum(jnp.where(triu, lf_col, 0.0), axis=0, keepdims=True)
        Dm = jnp.where(tril, F_col - F_row + ig_row, NEG_INF)
        m_prev = mN_ref[h:h + 1, 0:1]
        inter = F_col + m_prev
        m_t = jnp.maximum(jnp.max(Dm, axis=1, keepdims=True), inter)
        qb = q.astype(BF16)
        S = jnp.exp(Dm - m_t) * _dot_nt(qb, k.astype(BF16))
        w_inter = jnp.exp(inter - m_t)
        C = cN_ref[h]
        n_row = nN_ref[h:h + 1, :]
        num = (jnp.dot(S.astype(BF16), vb, preferred_element_type=F32)
               + w_inter * jnp.dot(qb, C.astype(BF16), preferred_element_type=F32))
        den = jnp.sum(S, axis=1, keepdims=True) + w_inter * jnp.sum(q * n_row, axis=1, keepdims=True)
        hh = num / jnp.maximum(jnp.abs(den), jnp.exp(-m_t))

        m_new = m_t[Lv - 1:Lv, :]
        F_last = F_col[Lv - 1:Lv, :]
        w_s = jnp.exp(F_last - F_col + ig_col - m_new)
        decay = jnp.exp(F_last + m_prev - m_new)
        kw = k * w_s
        cN_ref[h] = decay * C + jnp.dot(kw.T.astype(BF16), vb, preferred_element_type=F32)
        nN_ref[h:h + 1, :] = decay * n_row + jnp.sum(kw, axis=0, keepdims=True)
        mN_ref[h:h + 1, :] = jnp.broadcast_to(m_new, (1, LANES))

        hn = hh * lax.rsqrt(jnp.mean(hh * hh, axis=1, keepdims=True) + EPS) * mg_ref[:, hs]
        hn = hn[0:Lv, :]
        o = oz_ref[:, h * DH_M:(h + 1) * DH_M]
        z = oz_ref[:, W_M + h * DH_M:W_M + (h + 1) * DH_M]
        h_ref[:, hs] = (hn * _sigmoid(o) * _silu(z)).astype(h_ref.dtype)


def _mlstm(qkv, oz, gt, conv_w, conv_b, mnorm_g, state, out_dtype):
    B, T, _ = qkv.shape
    Lv = min(T, M_CHUNK)
    nc = T // Lv
    has_state = state is not None
    row = lambda b, c: (b, c, 0)
    fix2 = lambda b, c: (0, 0)
    per_b3 = lambda b, c: (b, 0, 0)
    per_b4 = lambda b, c: (b, 0, 0, 0)
    in_specs = [pl.BlockSpec((None, Lv, 3 * W_M), row),
                pl.BlockSpec((None, Lv, 2 * W_M), row),
                pl.BlockSpec((None, Lv, LANES), row),
                pl.BlockSpec((CONV_W, 2 * W_M), fix2),
                pl.BlockSpec((1, 2 * W_M), fix2),
                pl.BlockSpec((1, W_M), fix2)]
    args = [qkv, oz, gt, conv_w, conv_b.reshape(1, -1), mnorm_g.reshape(1, -1)]
    state_specs = [pl.BlockSpec((None, H_M, DH_M, DH_M), per_b4),
                   pl.BlockSpec((None, H_M, DH_M), per_b3),
                   pl.BlockSpec((None, H_M, LANES), per_b3),
                   pl.BlockSpec((None, CONV_W - 1, 2 * W_M), per_b3)]
    if has_state:
        in_specs += state_specs
        args += list(state)
    out_shape = [jax.ShapeDtypeStruct((B, T, W_M), out_dtype),
                 jax.ShapeDtypeStruct((B, H_M, DH_M, DH_M), F32),
                 jax.ShapeDtypeStruct((B, H_M, DH_M), F32),
                 jax.ShapeDtypeStruct((B, H_M, LANES), F32),
                 jax.ShapeDtypeStruct((B, CONV_W - 1, 2 * W_M), F32)]
    out_specs = [pl.BlockSpec((None, Lv, W_M), row)] + state_specs
    scratch = [pltpu.VMEM((SUBLANES + M_CHUNK, 2 * W_M), F32)]
    if Lv < M_CHUNK:
        scratch += [pltpu.VMEM((M_CHUNK, LANES), F32), pltpu.VMEM((M_CHUNK, W_M), F32)]
    return pl.pallas_call(
        functools.partial(_mlstm_body, Lv, has_state),
        grid=(B, nc), in_specs=in_specs, out_specs=out_specs, out_shape=out_shape,
        scratch_shapes=scratch, compiler_params=_cparams(("arbitrary", "arbitrary")), name="mlstm",
    )(*args)


def _fcum_body(n_chunks, g_ref, lff_ref, ft_ref):
    L = LANES
    triu = lax.broadcasted_iota(jnp.int32, (L, L), 0) <= lax.broadcasted_iota(jnp.int32, (L, L), 1)

    def chunk(c, carry):
        r0 = pl.multiple_of(c * L, L)
        LF = _log_sigmoid(g_ref[pl.ds(r0, L), :])
        lff_ref[pl.ds(r0, L), :] = LF
        new = []
        for h in range(H_F):
            col = LF[:, H_F + h:H_F + h + 1]
            fr = jnp.sum(jnp.where(triu, col, 0.0), axis=0, keepdims=True) + carry[h]
            ft_ref[c, h:h + 1, :] = fr
            new.append(fr[:, L - 1:L])
        return tuple(new)

    lax.fori_loop(0, n_chunks, chunk, tuple(jnp.zeros((1, 1), F32) for _ in range(H_F)))


def _fcum(gt):
    B, T, _ = gt.shape
    nc = T // LANES
    return pl.pallas_call(
        functools.partial(_fcum_body, nc),
        grid=(B,),
        in_specs=[pl.BlockSpec((None, T, LANES), lambda b: (b, 0, 0))],
        out_specs=[pl.BlockSpec((None, T, LANES), lambda b: (b, 0, 0)),
                   pl.BlockSpec((None, nc, H_F, LANES), lambda b: (b, 0, 0, 0))],
        out_shape=[jax.ShapeDtypeStruct((B, T, LANES), F32), jax.ShapeDtypeStruct((B, nc, H_F, LANES), F32)],
        compiler_params=_cparams(("arbitrary",)), name="fcum",
    )(gt)


def _fox_bias_layout(ft, tq):
    B, nc, _, L = ft.shape
    per = tq // L
    x = ft.reshape(B, nc // per, per, H_F // 2, 2, L)
    return x.transpose(0, 3, 1, 4, 2, 5).reshape(B, H_F // 2, nc // per, 2, tq)


def _lambda(lq1_ref, lk1_ref, lq2_ref, lk2_ref, lam_init):
    return (jnp.exp(jnp.sum(lq1_ref[...] * lk1_ref[...], axis=1, keepdims=True))
            - jnp.exp(jnp.sum(lq2_ref[...] * lk2_ref[...], axis=1, keepdims=True)) + lam_init)


def _attn_body(mode, tq, lam_init, q_ref, k_ref, v_ref, z_ref, *rest):
    if mode == "fox":
        ft_ref, o_ref = rest
    else:
        lq1, lk1, lq2, lk2, dg_ref, o_ref = rest
    qi = pl.program_id(2)
    lane = lax.broadcasted_iota(jnp.int32, (tq, LANES), 1)
    lane_row = lax.broadcasted_iota(jnp.int32, (1, LANES), 1)
    scale = DH_F ** -0.5
    q = q_ref[...]
    qs = (q * jnp.where(lane_row < DH_F, scale, 0.0).astype(BF16),
          q * jnp.where(lane_row >= DH_F, scale, 0.0).astype(BF16))

    def step(kv, carry, masked):
        k0 = pl.multiple_of(kv * tq, tq)
        k = k_ref[pl.ds(k0, tq), :]
        v = v_ref[pl.ds(k0, tq), :]
        out = []
        for a in range(2):
            m, l, acc = carry[a]
            s = _dot_nt(qs[a], k)
            if mode == "fox":
                s = s - ft_ref[kv, a:a + 1, :]
            if masked:
                r_i = lax.broadcasted_iota(jnp.int32, (tq, tq), 0)
                c_i = lax.broadcasted_iota(jnp.int32, (tq, tq), 1)
                s = jnp.where(r_i >= c_i, s, NEG_INF)
            m_new = jnp.maximum(m, jnp.max(s, axis=1, keepdims=True))
            p = jnp.exp(s - m_new)
            alpha = jnp.exp(m - m_new)
            l = alpha * l + jnp.sum(p, axis=1, keepdims=True)
            acc = alpha * acc + jnp.dot(p.astype(BF16), v, preferred_element_type=F32)
            out.append((m_new, l, acc))
        return tuple(out)

    init = tuple((jnp.full((tq, 1), NEG_INF, F32), jnp.zeros((tq, 1), F32), jnp.zeros((tq, LANES), F32))
                 for _ in range(2))
    carry = lax.fori_loop(0, qi, lambda kv, cr: step(kv, cr, False), init)
    (_, l0, a0), (_, l1, a1) = step(qi, carry, True)
    z = z_ref[...]
    if mode == "fox":
        o = jnp.where(lane < DH_F, a0 / l0, a1 / l1)
        o_ref[...] = (o * _silu(z)).astype(o_ref.dtype)
    else:
        lam = _lambda(lq1, lk1, lq2, lk2, lam_init)
        o = a0 / l0 - lam * (a1 / l1)
        o = o * lax.rsqrt(jnp.mean(o * o, axis=1, keepdims=True) + EPS) * dg_ref[...]
        o_ref[...] = (o * (1.0 - lam_init) * _silu(z)).astype(o_ref.dtype)


def _attn(mode, q, k, v, z, extra, lam_init, tq):
    B, T, C = q.shape
    G = C // LANES
    qspec = pl.BlockSpec((None, tq, LANES), lambda b, g, i: (b, i, g))
    kspec = pl.BlockSpec((None, T, LANES), lambda b, g, i: (b, 0, g))
    in_specs = [qspec, kspec, kspec, qspec]
    if mode == "fox":
        (ft,) = extra
        in_specs.append(pl.BlockSpec((None, None, T // tq, 2, tq), lambda b, g, i: (b, g, 0, 0, 0)))
        args = [q, k, v, z, ft]
    else:
        lq1, lk1, lq2, lk2, dg = extra
        in_specs += [pl.BlockSpec((1, DQ_D), lambda b, g, i: (0, 0))] * 4
        in_specs.append(pl.BlockSpec((1, DV_D), lambda b, g, i: (0, 0)))
        args = [q, k, v, z] + [a.reshape(1, -1) for a in (lq1, lk1, lq2, lk2, dg)]
    return pl.pallas_call(
        functools.partial(_attn_body, mode, tq, lam_init),
        grid=(B, G, T // tq), in_specs=in_specs, out_specs=qspec,
        out_shape=jax.ShapeDtypeStruct((B, T, C), BF16),
        compiler_params=_cparams(("arbitrary", "arbitrary", "arbitrary")), name="attn_" + mode,
    )(*args)


def _outproj_body(n_in, final, x_ref, *rest):
    a_refs, w_refs = rest[:n_in], rest[n_in:2 * n_in]
    rest = rest[2 * n_in:]
    acc = None
    for a_ref, w_ref in zip(a_refs, w_refs):
        t = jnp.dot(a_ref[...].astype(BF16), w_ref[...], preferred_element_type=F32)
        acc = t if acc is None else acc + t
    y = x_ref[...] + acc
    if final:
        g_ref, o_ref = rest
        y = y * lax.rsqrt(jnp.mean(y * y, axis=-1, keepdims=True) + EPS) * g_ref[...]
    else:
        (o_ref,) = rest
    o_ref[...] = y


def _outproj(x, acts, ws, final_g, tm):
    M, D = x.shape
    n_in = len(acts)
    final = final_g is not None
    in_specs = [pl.BlockSpec((tm, D), lambda i: (i, 0))]
    in_specs += [pl.BlockSpec((tm, a.shape[1]), lambda i: (i, 0)) for a in acts]
    in_specs += [pl.BlockSpec(w.shape, lambda i: (0, 0)) for w in ws]
    args = [x, *acts, *ws]
    if final:
        in_specs.append(pl.BlockSpec((1, D), lambda i: (0, 0)))
        args.append(final_g.reshape(1, D))
    return pl.pallas_call(
        functools.partial(_outproj_body, n_in, final),
        grid=(M // tm,), in_specs=in_specs, out_specs=pl.BlockSpec((tm, D), lambda i: (i, 0)),
        out_shape=jax.ShapeDtypeStruct((M, D), F32),
        compiler_params=_cparams(("arbitrary",)), name="outproj",
    )(*args)


def _softmax_step(s, m_s, l_s):
    m_old = m_s[...]
    m_new = jnp.maximum(m_old, jnp.max(s, axis=1, keepdims=True))
    pr = jnp.exp(s - m_new)
    alpha = jnp.exp(m_old - m_new)
    l_s[...] = alpha * l_s[...] + jnp.sum(pr, axis=1, keepdims=True)
    m_s[...] = m_new
    return pr.astype(BF16), alpha


def _decode_fox_body(R, Ts, pt_ref, q_ref, kn_ref, vn_ref, z_ref, g_ref, *refs):
    lf_refs, k_refs, v_refs = refs[:R], refs[R:2 * R], refs[2 * R:3 * R]
    o_ref, lffn_ref, qm, m_s, l_s, acc, carry, kpad, vpad, gpad = refs[3 * R:]
    P = LANES
    rows = H_F * SUBLANES
    p = pl.program_id(1)

    @pl.when(p == 0)
    def _():
        lane = lax.broadcasted_iota(jnp.int32, (Ts, W_F), 1)
        q = q_ref[...] * (DH_F ** -0.5)
        qm[...] = jnp.zeros(qm.shape, F32)
        for h in range(H_F):
            qm[h * SUBLANES:h * SUBLANES + Ts, :] = jnp.where((lane >= h * DH_F) & (lane < (h + 1) * DH_F), q, 0.0)
        m_s[...] = jnp.full(m_s.shape, NEG_INF, F32)
        l_s[...] = jnp.zeros(l_s.shape, F32)
        acc[...] = jnp.zeros(acc.shape, F32)
        carry[...] = jnp.zeros(carry.shape, F32)

    qb = qm[...].astype(BF16)
    lane_i = lax.broadcasted_iota(jnp.int32, (H_F, P), 1)

    def bias_rows(f):
        return jnp.concatenate([jnp.broadcast_to(f[h:h + 1, :], (SUBLANES, P)) for h in range(H_F)], axis=0)

    for r in range(R):
        f = lf_refs[r][...]
        shift = 1
        while shift < P:
            f = f + jnp.where(lane_i >= shift, pltpu.roll(f, shift, 1), 0.0)
            shift *= 2
        f = f + carry[...]
        carry[...] = jnp.broadcast_to(f[:, P - 1:P], (H_F, P))
        kT = k_refs[r][...].reshape(W_F, P).astype(BF16)
        vT = v_refs[r][...].reshape(W_F, P).astype(BF16)
        s = jnp.dot(qb, kT, preferred_element_type=F32) - bias_rows(f)
        pr, alpha = _softmax_step(s, m_s, l_s)
        acc[...] = alpha * acc[...] + _dot_nt(pr, vT)

    @pl.when(p == pl.num_programs(1) - 1)
    def _():
        lf = _log_sigmoid(g_ref[...])
        lffn_ref[...] = lf
        gpad[...] = jnp.zeros(gpad.shape, F32)
        gpad[0:Ts, :] = lf
        lfp = gpad[...]
        triu = lax.broadcasted_iota(jnp.int32, (P, P), 0) <= lax.broadcasted_iota(jnp.int32, (P, P), 1)
        fn = [jnp.sum(jnp.where(triu, lfp[:, 2 * H_M + h:2 * H_M + h + 1], 0.0), axis=0, keepdims=True)
              + carry[h:h + 1, :] for h in range(H_F)]
        bias = jnp.concatenate([jnp.broadcast_to(f, (SUBLANES, P)) for f in fn], axis=0)
        kpad[...] = jnp.zeros(kpad.shape, F32)
        kpad[0:Ts, :] = kn_ref[...]
        vpad[...] = jnp.zeros(vpad.shape, F32)
        vpad[0:Ts, :] = vn_ref[...]
        t_i = lax.broadcasted_iota(jnp.int32, (rows, P), 0) & (SUBLANES - 1)
        j_i = lax.broadcasted_iota(jnp.int32, (rows, P), 1)
        s = _dot_nt(qb, kpad[...].astype(BF16)) - bias
        s = jnp.where((j_i <= t_i) & (j_i < Ts), s, NEG_INF)
        pr, alpha = _softmax_step(s, m_s, l_s)
        o_full = (alpha * acc[...] + jnp.dot(pr, vpad[...].astype(BF16), preferred_element_type=F32)) / l_s[...]
        z = z_ref[...]
        lane = lax.broadcasted_iota(jnp.int32, (SUBLANES, LANES), 1)
        for j in range(W_F // LANES):
            cs = slice(j * LANES, (j + 1) * LANES)
            o = jnp.where(lane < DH_F, o_full[(2 * j) * SUBLANES:(2 * j + 1) * SUBLANES, cs],
                          o_full[(2 * j + 1) * SUBLANES:(2 * j + 2) * SUBLANES, cs])
            o_ref[:, cs] = o[0:Ts, :] * _silu(z[:, cs])


def _decode_fox(page_table, kT_cache, vT_cache, lfT_cache, layer, q, k_new, v_new, z, gt, R):
    Bs, n_pages = page_table.shape
    _, Ts, C = q.shape
    P = kT_cache.shape[-1]
    rows = H_F * SUBLANES
    tok = pl.BlockSpec((None, Ts, C), lambda b, p, pt: (b, 0, 0))
    gspec = pl.BlockSpec((None, Ts, LANES), lambda b, p, pt: (b, 0, 0))
    lf_pages = [pl.BlockSpec((None, None, H_F, P), lambda b, p, pt, r=r: (layer, pt[b, p * R + r], 0, 0))
                for r in range(R)]
    kv_pages = [pl.BlockSpec((None, None, H_F, DH_F, P), lambda b, p, pt, r=r: (layer, pt[b, p * R + r], 0, 0, 0))
                for r in range(R)]
    grid_spec = pltpu.PrefetchScalarGridSpec(
        num_scalar_prefetch=1, grid=(Bs, n_pages // R),
        in_specs=[tok, tok, tok, tok, gspec] + lf_pages + kv_pages + kv_pages, out_specs=[tok, gspec],
        scratch_shapes=[pltpu.VMEM((rows, C), F32), pltpu.VMEM((rows, 1), F32), pltpu.VMEM((rows, 1), F32),
                        pltpu.VMEM((rows, C), F32), pltpu.VMEM((H_F, P), F32),
                        pltpu.VMEM((P, C), F32), pltpu.VMEM((P, C), F32), pltpu.VMEM((P, LANES), F32)])
    return pl.pallas_call(
        functools.partial(_decode_fox_body, R, Ts), grid_spec=grid_spec,
        out_shape=[jax.ShapeDtypeStruct((Bs, Ts, C), F32), jax.ShapeDtypeStruct((Bs, Ts, LANES), F32)],
        compiler_params=_cparams(("arbitrary", "arbitrary")), name="decode_fox",
    )(page_table, q, k_new, v_new, z, gt, *([lfT_cache] * R), *([kT_cache] * R), *([vT_cache] * R))


def _decode_diff_body(R, Ts, lam_init, pt_ref, q_ref, kn_ref, vn_ref, z_ref, lq1, lk1, lq2, lk2, dg_ref, *refs):
    k_refs, v_refs = refs[:R], refs[R:2 * R]
    o_ref, qm, m_s, l_s, acc, maskb, kpad, vpad = refs[2 * R:]
    P = k_refs[0].shape[0]
    rows = 2 * H_D * SUBLANES
    grp = 2 * SUBLANES
    p = pl.program_id(1)

    @pl.when(p == 0)
    def _():
        lane = lax.broadcasted_iota(jnp.int32, (Ts, DV_D), 1)
        q = q_ref[...] * (DQ_D ** -0.5)
        qm[...] = jnp.zeros(qm.shape, F32)
        for h in range(H_D):
            qh = q[:, h * DV_D:(h + 1) * DV_D]
            qm[h * grp:h * grp + Ts, :] = jnp.where(lane < DQ_D, qh, 0.0)
            qm[h * grp + SUBLANES:h * grp + SUBLANES + Ts, :] = jnp.where(lane >= DQ_D, qh, 0.0)
        row_head = lax.shift_right_logical(lax.broadcasted_iota(jnp.int32, (rows, P * H_D), 0), grp.bit_length() - 1)
        col_head = lax.broadcasted_iota(jnp.int32, (rows, P * H_D), 1) & (H_D - 1)
        maskb[...] = jnp.where(row_head == col_head, 0.0, NEG_INF)
        m_s[...] = jnp.full(m_s.shape, NEG_INF, F32)
        l_s[...] = jnp.zeros(l_s.shape, F32)
        acc[...] = jnp.zeros(acc.shape, F32)

    qb = qm[...].astype(BF16)
    for r in range(R):
        kk = k_refs[r][...].reshape(P * H_D, DV_D).astype(BF16)
        vv = v_refs[r][...].reshape(P * H_D, DV_D).astype(BF16)
        pr, alpha = _softmax_step(_dot_nt(qb, kk) + maskb[...], m_s, l_s)
        acc[...] = alpha * acc[...] + jnp.dot(pr, vv, preferred_element_type=F32)

    @pl.when(p == pl.num_programs(1) - 1)
    def _():
        kpad[...] = jnp.zeros(kpad.shape, F32)
        kpad[0:Ts * H_D, :] = kn_ref[...]
        vpad[...] = jnp.zeros(vpad.shape, F32)
        vpad[0:Ts * H_D, :] = vn_ref[...]
        n_col = kpad.shape[0]
        r_i = lax.broadcasted_iota(jnp.int32, (rows, n_col), 0)
        j_i = lax.broadcasted_iota(jnp.int32, (rows, n_col), 1)
        key = lax.shift_right_logical(j_i, H_D.bit_length() - 1)
        row_head = lax.shift_right_logical(r_i, grp.bit_length() - 1)
        ok = ((j_i & (H_D - 1)) == row_head) & (key <= (r_i & (SUBLANES - 1))) & (key < Ts)
        s = jnp.where(ok, _dot_nt(qb, kpad[...].astype(BF16)), NEG_INF)
        pr, alpha = _softmax_step(s, m_s, l_s)
        o_full = (alpha * acc[...] + jnp.dot(pr, vpad[...].astype(BF16), preferred_element_type=F32)) / l_s[...]
        z = z_ref[...]
        lam = _lambda(lq1, lk1, lq2, lk2, lam_init)
        for h in range(H_D):
            cs = slice(h * DV_D, (h + 1) * DV_D)
            o = o_full[h * grp:h * grp + SUBLANES, :] - lam * o_full[h * grp + SUBLANES:(h + 1) * grp, :]
            o = o * lax.rsqrt(jnp.mean(o * o, axis=1, keepdims=True) + EPS) * dg_ref[...]
            o_ref[:, cs] = o[0:Ts, :] * (1.0 - lam_init) * _silu(z[:, cs])


def _decode_diff(page_table, k_cache, v_cache, layer, q, k_new, v_new, z, lam_params, lam_init, R):
    Bs, n_pages = page_table.shape
    _, Ts, C = q.shape
    P = k_cache.shape[2]
    rows = 2 * H_D * SUBLANES
    tok = pl.BlockSpec((None, Ts, C), lambda b, p, pt: (b, 0, 0))
    new = pl.BlockSpec((None, Ts * H_D, DV_D), lambda b, p, pt: (b, 0, 0))
    small = [pl.BlockSpec((1, DQ_D), lambda b, p, pt: (0, 0))] * 4 + [pl.BlockSpec((1, DV_D), lambda b, p, pt: (0, 0))]
    pages = [pl.BlockSpec((None, None, P, H_D, DV_D), lambda b, p, pt, r=r: (layer, pt[b, p * R + r], 0, 0, 0))
             for r in range(R)]
    grid_spec = pltpu.PrefetchScalarGridSpec(
        num_scalar_prefetch=1, grid=(Bs, n_pages // R),
        in_specs=[tok, new, new, tok] + small + pages + pages, out_specs=tok,
        scratch_shapes=[pltpu.VMEM((rows, DV_D), F32), pltpu.VMEM((rows, 1), F32), pltpu.VMEM((rows, 1), F32),
                        pltpu.VMEM((rows, DV_D), F32), pltpu.VMEM((rows, P * H_D), F32),
                        pltpu.VMEM((2 * SUBLANES * H_D, DV_D), F32), pltpu.VMEM((2 * SUBLANES * H_D, DV_D), F32)])
    return pl.pallas_call(
        functools.partial(_decode_diff_body, R, Ts, lam_init), grid_spec=grid_spec,
        out_shape=jax.ShapeDtypeStruct((Bs, Ts, C), F32),
        compiler_params=_cparams(("arbitrary", "arbitrary")), name="decode_diff",
    )(page_table, q, k_new, v_new, z, *[a.reshape(1, -1) for a in lam_params], *([k_cache] * R), *([v_cache] * R))


_EVEN_SRC = ((0, 5 * W_M), (5 * W_M + 2 * H_M, 5 * W_M + 2 * H_M + 4 * W_F),
             (5 * W_M, 5 * W_M + 2 * H_M), (5 * W_M + 2 * H_M + 4 * W_F, 5 * W_M + 2 * H_M + 4 * W_F + H_F))
_N_GATES = 2 * H_M + H_F


def _even_weights(w_in, b_in):
    pad = LANES - _N_GATES
    w = jnp.concatenate([w_in[:, a:b] for a, b in _EVEN_SRC] + [jnp.zeros((w_in.shape[0], pad), w_in.dtype)], axis=1)
    b = jnp.concatenate([b_in[a:b] for a, b in _EVEN_SRC] + [jnp.zeros((pad,), b_in.dtype)])
    return w.astype(BF16), b


def _even_plan(attn_dtype):
    kv = (F32,) if attn_dtype == F32 else (F32, attn_dtype)
    return ((0, 3 * W_M, (F32,)),
            (3 * W_M, 2 * W_M, (F32,)),
            (5 * W_M, W_F, (attn_dtype,)),
            (5 * W_M + W_F, W_F, kv),
            (5 * W_M + 2 * W_F, W_F, kv),
            (5 * W_M + 3 * W_F, W_F, (F32,)),
            (5 * W_M + 4 * W_F, LANES, (F32,)))


def _odd_plan(attn_dtype):
    kv = (F32,) if attn_dtype == F32 else (F32, attn_dtype)
    return ((0, W_D, (attn_dtype,)), (W_D, W_D, kv), (2 * W_D, W_D, kv), (3 * W_D, W_D, (F32,)))


def _tile(n, pref):
    t = min(n, pref)
    while n % t:
        t //= 2
    return t


def kernel(x_prompt, x_sample, cache_fox_k, cache_fox_v, cache_fox_logf, cache_diff_k, cache_diff_v, state_mlstm_c, state_mlstm_n, state_mlstm_m, state_mlstm_conv, page_table, norm_g, final_norm_g, w_in_even, b_in_even, conv_w, conv_b, mlstm_norm_g, w_out_even, w_in_odd, lambda_q1, lambda_k1, lambda_q2, lambda_k2, diff_norm_g, w_out_odd):
    B, T, D = x_prompt.shape
    Bs, Ts, _ = x_sample.shape
    depth = norm_g.shape[0]
    n_pool, P = cache_fox_k.shape[1], cache_fox_k.shape[2]
    n_pages = page_table.shape[1]
    Mp, Ms = B * T, Bs * Ts
    tm_p, tm_s = _tile(Mp, 256), _tile(Ms, 256)
    tq = _tile(T, 256)
    R = _tile(n_pages, 8)
    fox_kT_cache = jnp.transpose(cache_fox_k, (0, 1, 3, 4, 2))
    fox_vT_cache = jnp.transpose(cache_fox_v, (0, 1, 3, 4, 2))
    fox_lfT_cache = jnp.transpose(cache_fox_logf, (0, 1, 3, 2))

    hp = x_prompt.reshape(Mp, D)
    hs = x_sample.reshape(Ms, D)
    names = ("fox_k", "fox_v", "fox_logf", "diff_k", "diff_v", "mlstm_c", "mlstm_n", "mlstm_m", "mlstm_conv")
    outs_p = {nm: [] for nm in names}
    outs_s = {nm: [] for nm in names}

    for l in range(depth):
        i = l // 2
        last = l == depth - 1
        fg = final_norm_g if last else None
        if l % 2 == 0:
            w, b = _even_weights(w_in_even[i], b_in_even[i])
            wo = w_out_even[i].astype(BF16)
            wo_m, wo_f = wo[:W_M], wo[W_M:]
            qkv, oz, qf, kf, kfb, vf, vfb, zf, gt = _proj(hp, norm_g[l], w, b, _even_plan(BF16), tm_p)
            hm, c_p, n_p, m_p, cv_p = _mlstm(qkv.reshape(B, T, -1), oz.reshape(B, T, -1), gt.reshape(B, T, -1),
                                             conv_w[i], conv_b[i], mlstm_norm_g[i], None, BF16)
            lff, ft = _fcum(gt.reshape(B, T, LANES))
            hf = _attn("fox", qf.reshape(B, T, W_F), kfb.reshape(B, T, W_F), vfb.reshape(B, T, W_F),
                       zf.reshape(B, T, W_F), (_fox_bias_layout(ft, tq),), 0.0, tq)
            hp = _outproj(hp, [hm.reshape(Mp, W_M), hf.reshape(Mp, W_F)], [wo_m, wo_f], fg, tm_p)
            outs_p["fox_k"].append(kf.reshape(B, T, H_F, DH_F))
            outs_p["fox_v"].append(vf.reshape(B, T, H_F, DH_F))
            outs_p["fox_logf"].append(lff[:, :, 2 * H_M:2 * H_M + H_F])
            outs_p["mlstm_c"].append(c_p)
            outs_p["mlstm_n"].append(n_p)
            outs_p["mlstm_m"].append(m_p[:, :, 0])
            outs_p["mlstm_conv"].append(cv_p)
            qkv, oz, qf, kf, vf, zf, gt = _proj(hs, norm_g[l], w, b, _even_plan(F32), tm_s)
            state = (state_mlstm_c[i], state_mlstm_n[i],
                     jnp.broadcast_to(state_mlstm_m[i][:, :, None], (Bs, H_M, LANES)), state_mlstm_conv[i])
            hm, c_s, n_s, m_s, cv_s = _mlstm(qkv.reshape(Bs, Ts, -1), oz.reshape(Bs, Ts, -1), gt.reshape(Bs, Ts, -1),
                                             conv_w[i], conv_b[i], mlstm_norm_g[i], state, F32)
            hf, lffn = _decode_fox(page_table, fox_kT_cache, fox_vT_cache, fox_lfT_cache, i, qf.reshape(Bs, Ts, W_F),
                                   kf.reshape(Bs, Ts, W_F), vf.reshape(Bs, Ts, W_F), zf.reshape(Bs, Ts, W_F),
                                   gt.reshape(Bs, Ts, LANES), R)
            hs = _outproj(hs, [hm.reshape(Ms, W_M), hf.reshape(Ms, W_F)], [wo_m, wo_f], fg, tm_s)
            outs_s["fox_k"].append(kf.reshape(Bs, Ts, H_F, DH_F))
            outs_s["fox_v"].append(vf.reshape(Bs, Ts, H_F, DH_F))
            outs_s["fox_logf"].append(lffn[:, :, 2 * H_M:2 * H_M + H_F])
            outs_s["mlstm_c"].append(c_s)
            outs_s["mlstm_n"].append(n_s)
            outs_s["mlstm_m"].append(m_s[:, :, 0])
            outs_s["mlstm_conv"].append(cv_s)
        else:
            lam_init = 0.8 - 0.6 * float(np.exp(-0.3 * l))
            w = w_in_odd[i].astype(BF16)
            wo = w_out_odd[i].astype(BF16)
            lam_p = (lambda_q1[i], lambda_k1[i], lambda_q2[i], lambda_k2[i], diff_norm_g[i])
            q, k, kb, v, vb, z = _proj(hp, norm_g[l], w, None, _odd_plan(BF16), tm_p)
            o = _attn("diff", q.reshape(B, T, W_D), kb.reshape(B, T, W_D), vb.reshape(B, T, W_D),
                      z.reshape(B, T, W_D), lam_p, lam_init, tq)
            hp = _outproj(hp, [o.reshape(Mp, W_D)], [wo], fg, tm_p)
            outs_p["diff_k"].append(k.reshape(B, T, H_D, 2 * DQ_D))
            outs_p["diff_v"].append(v.reshape(B, T, H_D, DV_D))
            q, k, v, z = _proj(hs, norm_g[l], w, None, _odd_plan(F32), tm_s)
            o = _decode_diff(page_table, cache_diff_k, cache_diff_v, i, q.reshape(Bs, Ts, W_D),
                             k.reshape(Bs, Ts * H_D, DV_D), v.reshape(Bs, Ts * H_D, DV_D), z.reshape(Bs, Ts, W_D),
                             lam_p, lam_init, R)
            hs = _outproj(hs, [o.reshape(Ms, W_D)], [wo], fg, tm_s)
            outs_s["diff_k"].append(k.reshape(Bs, Ts, H_D, 2 * DQ_D))
            outs_s["diff_v"].append(v.reshape(Bs, Ts, H_D, DV_D))

    return (hp.reshape(B, T, D), hs.reshape(Bs, Ts, D),
            *[jnp.stack(outs_p[nm]) for nm in names], *[jnp.stack(outs_s[nm]) for nm in names])
```

```python
import functools

import numpy as np
import jax
import jax.numpy as jnp
from jax import lax
from jax.experimental import pallas as pl
from jax.experimental.pallas import tpu as pltpu

F32 = jnp.float32
BF16 = jnp.bfloat16
NEG_INF = float("-inf")

EPS = 1e-6
LOG2E = 1.4426950408889634
H_M, DH_M = 4, 128
H_F, DH_F = 8, 64
H_D, DV_D, DQ_D = 8, 128, 64
CONV_W = 4
M_CHUNK = 128
LANES = 128
SUBLANES = 8
VMEM_LIMIT = 52 * 1024 * 1024

W_M = H_M * DH_M
W_F = H_F * DH_F
W_D = H_D * DV_D


def _cparams(sem):
    return pltpu.CompilerParams(dimension_semantics=sem, vmem_limit_bytes=VMEM_LIMIT)


def _sigmoid(x):
    return 1.0 / (1.0 + jnp.exp(-x))


def _silu(x):
    return x * _sigmoid(x)


def _log_sigmoid(x):
    return jnp.minimum(x, 0.0) - jnp.log(1.0 + jnp.exp(-jnp.abs(x)))


def _dot_nt(a, b):
    return lax.dot_general(a, b, (((1,), (1,)), ((), ())), preferred_element_type=F32)


def _proj_body(plan, has_bias, x_ref, g_ref, w_ref, *rest):
    if has_bias:
        b_ref, *out_refs = rest
    else:
        out_refs = rest
    x = x_ref[...]
    xn = x * lax.rsqrt(jnp.mean(x * x, axis=-1, keepdims=True) + EPS) * g_ref[...]
    xn = xn.astype(BF16)
    k = 0
    for c0, width, dtypes, scale in plan:
        y = jnp.dot(xn, w_ref[:, c0:c0 + width], preferred_element_type=F32)
        if has_bias:
            y = y + b_ref[:, c0:c0 + width]
        if scale != 1.0:
            y = y * scale
        for dt in dtypes:
            out_refs[k][...] = y.astype(dt)
            k += 1


def _proj(x, g, w, b, plan, tm):
    M, D = x.shape
    N = w.shape[1]
    has_bias = b is not None
    out_shape, out_specs = [], []
    for _, width, dtypes, _ in plan:
        for dt in dtypes:
            out_shape.append(jax.ShapeDtypeStruct((M, width), dt))
            out_specs.append(pl.BlockSpec((tm, width), lambda i: (i, 0)))
    in_specs = [pl.BlockSpec((tm, D), lambda i: (i, 0)),
                pl.BlockSpec((1, D), lambda i: (0, 0)),
                pl.BlockSpec((D, N), lambda i: (0, 0))]
    args = [x, g.reshape(1, D), w]
    if has_bias:
        in_specs.append(pl.BlockSpec((1, N), lambda i: (0, 0)))
        args.append(b.reshape(1, N))
    return pl.pallas_call(
        functools.partial(_proj_body, plan, has_bias),
        grid=(M // tm,), in_specs=in_specs, out_specs=out_specs, out_shape=out_shape,
        compiler_params=_cparams(("arbitrary",)), name="proj",
    )(*args)


def _mlstm_body(Lv, has_state, qkv_ref, oz_ref, gt_ref, cw_ref, cb_ref, mg_ref, *rest):
    L = M_CHUNK
    if has_state:
        c0_ref, n0_ref, m0_ref, cv0_ref, *rest = rest
    h_ref, cN_ref, nN_ref, mN_ref, cvN_ref, ubuf, *pads = rest
    c = pl.program_id(1)

    @pl.when(c == 0)
    def _():
        ubuf[...] = jnp.zeros(ubuf.shape, F32)
        if has_state:
            ubuf[SUBLANES - (CONV_W - 1):SUBLANES, :] = cv0_ref[...]
            cN_ref[...] = c0_ref[...]
            nN_ref[...] = n0_ref[...]
            mN_ref[...] = m0_ref[...]
        else:
            cN_ref[...] = jnp.zeros(cN_ref.shape, F32)
            nN_ref[...] = jnp.zeros(nN_ref.shape, F32)
            mN_ref[...] = jnp.zeros(mN_ref.shape, F32)

    ubuf[SUBLANES:SUBLANES + Lv, :] = qkv_ref[:, 0:2 * W_M]
    acc = None
    for j in range(CONV_W):
        r0 = SUBLANES - (CONV_W - 1) + j
        term = ubuf[r0:r0 + L, :] * cw_ref[j:j + 1, :]
        acc = term if acc is None else acc + term
    y = acc + cb_ref[...]
    tail = ubuf[SUBLANES + Lv - (CONV_W - 1):SUBLANES + Lv, :]
    ubuf[SUBLANES - (CONV_W - 1):SUBLANES, :] = tail
    cvN_ref[...] = tail
    qk = _silu(y)

    if Lv == L:
        G = gt_ref[...]
        V = qkv_ref[:, 2 * W_M:3 * W_M]
    else:
        gbuf, vbuf = pads
        gbuf[...] = jnp.zeros(gbuf.shape, F32)
        gbuf[0:Lv, :] = gt_ref[...]
        vbuf[...] = jnp.zeros(vbuf.shape, F32)
        vbuf[0:Lv, :] = qkv_ref[:, 2 * W_M:3 * W_M]
        G = gbuf[...]
        V = vbuf[...]
    LF = _log_sigmoid(G)
    GT = G.T
    LFT = LF.T
    r_i = lax.broadcasted_iota(jnp.int32, (L, L), 0)
    c_i = lax.broadcasted_iota(jnp.int32, (L, L), 1)
    tril = r_i >= c_i
    triu = r_i <= c_i
    rowv = lax.broadcasted_iota(jnp.int32, (L, 1), 0) < Lv
    colv = lax.broadcasted_iota(jnp.int32, (1, L), 1) < Lv

    for h in range(H_M):
        hs = slice(h * DH_M, (h + 1) * DH_M)
        q = qk[:, h * DH_M:(h + 1) * DH_M]
        k = qk[:, W_M + h * DH_M:W_M + (h + 1) * DH_M] * (DH_M ** -0.5)
        vb = V[:, hs].astype(BF16)
        ig_row = GT[h:h + 1, :]
        lf_row = LFT[H_M + h:H_M + h + 1, :]
        ig_col = G[:, h:h + 1]
        lf_col = LF[:, H_M + h:H_M + h + 1]
        if Lv < L:
            lf_row = jnp.where(colv, lf_row, 0.0)
            lf_col = jnp.where(rowv, lf_col, 0.0)
            ig_row = jnp.where(colv, ig_row, NEG_INF)
            ig_col = jnp.where(rowv, ig_col, NEG_INF)
        F_col = jnp.sum(jnp.where(tril, lf_row, 0.0), axis=1, keepdims=True)
        F_row = jnp.sum(jnp.where(triu, lf_col, 0.0), axis=0, keepdims=True)
        Dm = jnp.where(tril, F_col - F_row + ig_row, NEG_INF)
        m_prev = mN_ref[h:h + 1, 0:1]
        inter = F_col + m_prev
        m_t = jnp.maximum(jnp.max(Dm, axis=1, keepdims=True), inter)
        qb = q.astype(BF16)
        S = jnp.exp(Dm - m_t) * _dot_nt(qb, k.astype(BF16))
        w_inter = jnp.exp(inter - m_t)
        C = cN_ref[h]
        n_row = nN_ref[h:h + 1, :]
        num = (jnp.dot(S.astype(BF16), vb, preferred_element_type=F32)
               + w_inter * jnp.dot(qb, C.astype(BF16), preferred_element_type=F32))
        den = jnp.sum(S, axis=1, keepdims=True) + w_inter * jnp.sum(q * n_row, axis=1, keepdims=True)
        hh = num / jnp.maximum(jnp.abs(den), jnp.exp(-m_t))

        m_new = m_t[Lv - 1:Lv, :]
        F_last = F_col[Lv - 1:Lv, :]
        w_s = jnp.exp(F_last - F_col + ig_col - m_new)
        decay = jnp.exp(F_last + m_prev - m_new)
        kw = k * w_s
        cN_ref[h] = decay * C + jnp.dot(kw.T.astype(BF16), vb, preferred_element_type=F32)
        nN_ref[h:h + 1, :] = decay * n_row + jnp.sum(kw, axis=0, keepdims=True)
        mN_ref[h:h + 1, :] = jnp.broadcast_to(m_new, (1, LANES))

        hn = hh * lax.rsqrt(jnp.mean(hh * hh, axis=1, keepdims=True) + EPS) * mg_ref[:, hs]
        hn = hn[0:Lv, :]
        o = oz_ref[:, h * DH_M:(h + 1) * DH_M]
        z = oz_ref[:, W_M + h * DH_M:W_M + (h + 1) * DH_M]
        h_ref[:, hs] = (hn * _sigmoid(o) * _silu(z)).astype(h_ref.dtype)


def _mlstm(qkv, oz, gt, conv_w, conv_b, mnorm_g, state, out_dtype):
    B, T, _ = qkv.shape
    Lv = min(T, M_CHUNK)
    nc = T // Lv
    has_state = state is not None
    row = lambda b, c: (b, c, 0)
    fix2 = lambda b, c: (0, 0)
    per_b3 = lambda b, c: (b, 0, 0)
    per_b4 = lambda b, c: (b, 0, 0, 0)
    in_specs = [pl.BlockSpec((None, Lv, 3 * W_M), row),
                pl.BlockSpec((None, Lv, 2 * W_M), row),
                pl.BlockSpec((None, Lv, LANES), row),
                pl.BlockSpec((CONV_W, 2 * W_M), fix2),
                pl.BlockSpec((1, 2 * W_M), fix2),
                pl.BlockSpec((1, W_M), fix2)]
    args = [qkv, oz, gt, conv_w, conv_b.reshape(1, -1), mnorm_g.reshape(1, -1)]
    state_specs = [pl.BlockSpec((None, H_M, DH_M, DH_M), per_b4),
                   pl.BlockSpec((None, H_M, DH_M), per_b3),
                   pl.BlockSpec((None, H_M, LANES), per_b3),
                   pl.BlockSpec((None, CONV_W - 1, 2 * W_M), per_b3)]
    if has_state:
        in_specs += state_specs
        args += list(state)
    out_shape = [jax.ShapeDtypeStruct((B, T, W_M), out_dtype),
                 jax.ShapeDtypeStruct((B, H_M, DH_M, DH_M), F32),
                 jax.ShapeDtypeStruct((B, H_M, DH_M), F32),
                 jax.ShapeDtypeStruct((B, H_M, LANES), F32),
                 jax.ShapeDtypeStruct((B, CONV_W - 1, 2 * W_M), F32)]
    out_specs = [pl.BlockSpec((None, Lv, W_M), row)] + state_specs
    scratch = [pltpu.VMEM((SUBLANES + M_CHUNK, 2 * W_M), F32)]
    if Lv < M_CHUNK:
        scratch += [pltpu.VMEM((M_CHUNK, LANES), F32), pltpu.VMEM((M_CHUNK, W_M), F32)]
    return pl.pallas_call(
        functools.partial(_mlstm_body, Lv, has_state),
        grid=(B, nc), in_specs=in_specs, out_specs=out_specs, out_shape=out_shape,
        scratch_shapes=scratch, compiler_params=_cparams(("arbitrary", "arbitrary")), name="mlstm",
    )(*args)


def _fcum_body(n_chunks, g_ref, lff_ref, ft_ref):
    L = LANES
    triu = lax.broadcasted_iota(jnp.int32, (L, L), 0) <= lax.broadcasted_iota(jnp.int32, (L, L), 1)

    def chunk(c, carry):
        r0 = pl.multiple_of(c * L, L)
        LF = _log_sigmoid(g_ref[pl.ds(r0, L), :])
        lff_ref[pl.ds(r0, L), :] = LF
        new = []
        for h in range(H_F):
            col = LF[:, H_F + h:H_F + h + 1]
            fr = jnp.sum(jnp.where(triu, col, 0.0), axis=0, keepdims=True) + carry[h]
            ft_ref[c, h:h + 1, :] = fr * LOG2E
            new.append(fr[:, L - 1:L])
        return tuple(new)

    lax.fori_loop(0, n_chunks, chunk, tuple(jnp.zeros((1, 1), F32) for _ in range(H_F)))


def _fcum(gt):
    B, T, _ = gt.shape
    nc = T // LANES
    return pl.pallas_call(
        functools.partial(_fcum_body, nc),
        grid=(B,),
        in_specs=[pl.BlockSpec((None, T, LANES), lambda b: (b, 0, 0))],
        out_specs=[pl.BlockSpec((None, T, LANES), lambda b: (b, 0, 0)),
                   pl.BlockSpec((None, nc, H_F, LANES), lambda b: (b, 0, 0, 0))],
        out_shape=[jax.ShapeDtypeStruct((B, T, LANES), F32), jax.ShapeDtypeStruct((B, nc, H_F, LANES), F32)],
        compiler_params=_cparams(("arbitrary",)), name="fcum",
    )(gt)


def _fox_bias_layout(ft, tq):
    B, nc, _, L = ft.shape
    per = tq // L
    x = ft.reshape(B, nc // per, per, H_F // 2, 2, L)
    return x.transpose(0, 3, 1, 4, 2, 5).reshape(B, H_F // 2, nc // per, 2, tq)


def _lambda(lq1_ref, lk1_ref, lq2_ref, lk2_ref, lam_init):
    return (jnp.exp(jnp.sum(lq1_ref[...] * lk1_ref[...], axis=1, keepdims=True))
            - jnp.exp(jnp.sum(lq2_ref[...] * lk2_ref[...], axis=1, keepdims=True)) + lam_init)


def _attn_body(mode, tq, lam_init, q_ref, k_ref, v_ref, z_ref, *rest):
    if mode == "fox":
        ft_ref, o_ref, q2, p_scr, m_s, acc = rest
    else:
        lq1, lk1, lq2, lk2, dg_ref, o_ref, q2, p_scr, m_s, acc = rest
    qi = pl.program_id(2)
    rows = 2 * tq
    lane_row = lax.broadcasted_iota(jnp.int32, (1, LANES), 1)
    q = q_ref[...]
    q2[0:tq, :] = q * jnp.where(lane_row < DH_F, 1.0, 0.0).astype(BF16)
    q2[tq:rows, :] = q * jnp.where(lane_row >= DH_F, 1.0, 0.0).astype(BF16)
    m_s[...] = jnp.full(m_s.shape, NEG_INF, F32)
    acc[...] = jnp.zeros(acc.shape, F32)
    p_scr[...] = jnp.zeros(p_scr.shape, BF16)
    ones = jnp.ones((tq, LANES), BF16)

    def scores(j):
        k0 = pl.multiple_of(j * tq, tq)
        return _dot_nt(q2[...], k_ref[pl.ds(k0, tq), :])

    def pv(j):
        k0 = pl.multiple_of(j * tq, tq)
        vext = jnp.concatenate([v_ref[pl.ds(k0, tq), :], ones], axis=1)
        return jnp.dot(p_scr[...], vext, preferred_element_type=F32)

    def softmax(s, j, masked):
        if mode == "fox":
            s = jnp.concatenate([s[0:tq] - ft_ref[j, 0:1, :], s[tq:rows] - ft_ref[j, 1:2, :]], axis=0)
        if masked:
            r_i = lax.broadcasted_iota(jnp.int32, (rows, tq), 0) & (tq - 1)
            c_i = lax.broadcasted_iota(jnp.int32, (rows, tq), 1)
            s = jnp.where(r_i >= c_i, s, NEG_INF)
        m_old = m_s[...]
        m_new = jnp.maximum(m_old, jnp.max(s, axis=1, keepdims=True))
        p = jnp.exp2(s - jnp.concatenate([m_new] * (tq // LANES), axis=1)).astype(BF16)
        alpha = jnp.exp2(m_old - m_new)
        m_s[...] = m_new
        return p, jnp.concatenate([alpha, alpha], axis=1)

    def body(j, c):
        pend = pv(jnp.maximum(j - 1, 0))
        p, alpha = softmax(scores(j), j, False)
        acc[...] = alpha * (acc[...] + pend)
        p_scr[...] = p
        return c

    lax.fori_loop(0, qi, body, 0)
    pend = pv(jnp.maximum(qi - 1, 0))
    p, alpha = softmax(scores(qi), qi, True)
    a = alpha * (acc[...] + pend)
    p_scr[...] = p
    a = a + pv(qi)
    o0 = a[0:tq, 0:LANES] / a[0:tq, LANES:2 * LANES]
    o1 = a[tq:rows, 0:LANES] / a[tq:rows, LANES:2 * LANES]
    z = z_ref[...]
    if mode == "fox":
        lane = lax.broadcasted_iota(jnp.int32, (tq, LANES), 1)
        o = jnp.where(lane < DH_F, o0, o1)
        o_ref[...] = (o * _silu(z)).astype(o_ref.dtype)
    else:
        lam = _lambda(lq1, lk1, lq2, lk2, lam_init)
        o = o0 - lam * o1
        o = o * lax.rsqrt(jnp.mean(o * o, axis=1, keepdims=True) + EPS) * dg_ref[...]
        o_ref[...] = (o * (1.0 - lam_init) * _silu(z)).astype(o_ref.dtype)


def _attn(mode, q, k, v, z, extra, lam_init, tq):
    B, T, C = q.shape
    G = C // LANES
    scratch = [pltpu.VMEM((2 * tq, LANES), BF16), pltpu.VMEM((2 * tq, tq), BF16),
               pltpu.VMEM((2 * tq, LANES), F32), pltpu.VMEM((2 * tq, 2 * LANES), F32)]
    qspec = pl.BlockSpec((None, tq, LANES), lambda b, g, i: (b, i, g))
    kspec = pl.BlockSpec((None, T, LANES), lambda b, g, i: (b, 0, g))
    in_specs = [qspec, kspec, kspec, qspec]
    if mode == "fox":
        (ft,) = extra
        in_specs.append(pl.BlockSpec((None, None, T // tq, 2, tq), lambda b, g, i: (b, g, 0, 0, 0)))
        args = [q, k, v, z, ft]
    else:
        lq1, lk1, lq2, lk2, dg = extra
        in_specs += [pl.BlockSpec((1, DQ_D), lambda b, g, i: (0, 0))] * 4
        in_specs.append(pl.BlockSpec((1, DV_D), lambda b, g, i: (0, 0)))
        args = [q, k, v, z] + [a.reshape(1, -1) for a in (lq1, lk1, lq2, lk2, dg)]
    return pl.pallas_call(
        functools.partial(_attn_body, mode, tq, lam_init),
        grid=(B, G, T // tq), in_specs=in_specs, out_specs=qspec,
        out_shape=jax.ShapeDtypeStruct((B, T, C), BF16), scratch_shapes=scratch,
        compiler_params=_cparams(("arbitrary", "arbitrary", "arbitrary")), name="attn_" + mode,
    )(*args)


def _outproj_body(n_in, final, x_ref, *rest):
    a_refs, w_refs = rest[:n_in], rest[n_in:2 * n_in]
    rest = rest[2 * n_in:]
    acc = None
    for a_ref, w_ref in zip(a_refs, w_refs):
        t = jnp.dot(a_ref[...].astype(BF16), w_ref[...], preferred_element_type=F32)
        acc = t if acc is None else acc + t
    y = x_ref[...] + acc
    if final:
        g_ref, o_ref = rest
        y = y * lax.rsqrt(jnp.mean(y * y, axis=-1, keepdims=True) + EPS) * g_ref[...]
    else:
        (o_ref,) = rest
    o_ref[...] = y


def _outproj(x, acts, ws, final_g, tm):
    M, D = x.shape
    n_in = len(acts)
    final = final_g is not None
    in_specs = [pl.BlockSpec((tm, D), lambda i: (i, 0))]
    in_specs += [pl.BlockSpec((tm, a.shape[1]), lambda i: (i, 0)) for a in acts]
    in_specs += [pl.BlockSpec(w.shape, lambda i: (0, 0)) for w in ws]
    args = [x, *acts, *ws]
    if final:
        in_specs.append(pl.BlockSpec((1, D), lambda i: (0, 0)))
        args.append(final_g.reshape(1, D))
    return pl.pallas_call(
        functools.partial(_outproj_body, n_in, final),
        grid=(M // tm,), in_specs=in_specs, out_specs=pl.BlockSpec((tm, D), lambda i: (i, 0)),
        out_shape=jax.ShapeDtypeStruct((M, D), F32),
        compiler_params=_cparams(("arbitrary",)), name="outproj",
    )(*args)


def _softmax_step(s, m_s, l_s):
    m_old = m_s[...]
    m_new = jnp.maximum(m_old, jnp.max(s, axis=1, keepdims=True))
    pr = jnp.exp(s - m_new)
    alpha = jnp.exp(m_old - m_new)
    l_s[...] = alpha * l_s[...] + jnp.sum(pr, axis=1, keepdims=True)
    m_s[...] = m_new
    return pr.astype(BF16), alpha


def _decode_fox_body(R, Ts, pt_ref, q_ref, kn_ref, vn_ref, z_ref, g_ref, *refs):
    lf_refs, k_refs, v_refs = refs[:R], refs[R:2 * R], refs[2 * R:3 * R]
    o_ref, lffn_ref, qm, m_s, l_s, acc, carry, kpad, vpad, gpad = refs[3 * R:]
    P = LANES
    rows = H_F * SUBLANES
    p = pl.program_id(1)

    @pl.when(p == 0)
    def _():
        lane = lax.broadcasted_iota(jnp.int32, (Ts, W_F), 1)
        q = q_ref[...] * (DH_F ** -0.5)
        qm[...] = jnp.zeros(qm.shape, F32)
        for h in range(H_F):
            qm[h * SUBLANES:h * SUBLANES + Ts, :] = jnp.where((lane >= h * DH_F) & (lane < (h + 1) * DH_F), q, 0.0)
        m_s[...] = jnp.full(m_s.shape, NEG_INF, F32)
        l_s[...] = jnp.zeros(l_s.shape, F32)
        acc[...] = jnp.zeros(acc.shape, F32)
        carry[...] = jnp.zeros(carry.shape, F32)

    qb = qm[...].astype(BF16)

    def bias_rows(f):
        return jnp.concatenate([jnp.broadcast_to(f[h:h + 1, :], (SUBLANES, P)) for h in range(H_F)], axis=0)

    lf = jnp.concatenate([lf_refs[r][...] for r in range(R)], axis=0)
    tri = (lax.broadcasted_iota(jnp.int32, (P, P), 0) <= lax.broadcasted_iota(jnp.int32, (P, P), 1))
    tri = jnp.where(tri, 1.0, 0.0).astype(BF16)
    hi = lf.astype(BF16)
    rem = lf - hi.astype(F32)
    mid = rem.astype(BF16)
    lo = (rem - mid.astype(F32)).astype(BF16)
    f = (jnp.dot(hi, tri, preferred_element_type=F32) + jnp.dot(mid, tri, preferred_element_type=F32)
         + jnp.dot(lo, tri, preferred_element_type=F32))
    totals = [jnp.broadcast_to(f[r * H_F:(r + 1) * H_F, P - 1:P], (H_F, P)) for r in range(R)]
    run = carry[...]
    s_parts = []
    for r in range(R):
        kT = k_refs[r][...].reshape(W_F, P).astype(BF16)
        s_parts.append(jnp.dot(qb, kT, preferred_element_type=F32) - bias_rows(f[r * H_F:(r + 1) * H_F, :] + run))
        run = run + totals[r]
    carry[...] = run
    pr, alpha = _softmax_step(jnp.concatenate(s_parts, axis=1), m_s, l_s)
    pv = None
    for r in range(R):
        vT = v_refs[r][...].reshape(W_F, P).astype(BF16)
        t = _dot_nt(pr[:, r * P:(r + 1) * P], vT)
        pv = t if pv is None else pv + t
    acc[...] = alpha * acc[...] + pv

    @pl.when(p == pl.num_programs(1) - 1)
    def _():
        lf = _log_sigmoid(g_ref[...])
        lffn_ref[...] = lf
        gpad[...] = jnp.zeros(gpad.shape, F32)
        gpad[0:Ts, :] = lf
        lfp = gpad[...]
        triu = lax.broadcasted_iota(jnp.int32, (P, P), 0) <= lax.broadcasted_iota(jnp.int32, (P, P), 1)
        fn = [jnp.sum(jnp.where(triu, lfp[:, 2 * H_M + h:2 * H_M + h + 1], 0.0), axis=0, keepdims=True)
              + carry[h:h + 1, :] for h in range(H_F)]
        bias = jnp.concatenate([jnp.broadcast_to(f, (SUBLANES, P)) for f in fn], axis=0)
        kpad[...] = jnp.zeros(kpad.shape, F32)
        kpad[0:Ts, :] = kn_ref[...]
        vpad[...] = jnp.zeros(vpad.shape, F32)
        vpad[0:Ts, :] = vn_ref[...]
        t_i = lax.broadcasted_iota(jnp.int32, (rows, P), 0) & (SUBLANES - 1)
        j_i = lax.broadcasted_iota(jnp.int32, (rows, P), 1)
        s = _dot_nt(qb, kpad[...].astype(BF16)) - bias
        s = jnp.where((j_i <= t_i) & (j_i < Ts), s, NEG_INF)
        pr, alpha = _softmax_step(s, m_s, l_s)
        o_full = (alpha * acc[...] + jnp.dot(pr, vpad[...].astype(BF16), preferred_element_type=F32)) / l_s[...]
        z = z_ref[...]
        lane = lax.broadcasted_iota(jnp.int32, (SUBLANES, LANES), 1)
        for j in range(W_F // LANES):
            cs = slice(j * LANES, (j + 1) * LANES)
            o = jnp.where(lane < DH_F, o_full[(2 * j) * SUBLANES:(2 * j + 1) * SUBLANES, cs],
                          o_full[(2 * j + 1) * SUBLANES:(2 * j + 2) * SUBLANES, cs])
            o_ref[:, cs] = o[0:Ts, :] * _silu(z[:, cs])


def _decode_fox(page_table, kT_cache, vT_cache, lfT_cache, layer, q, k_new, v_new, z, gt, R):
    Bs, n_pages = page_table.shape
    _, Ts, C = q.shape
    P = kT_cache.shape[-1]
    rows = H_F * SUBLANES
    tok = pl.BlockSpec((None, Ts, C), lambda b, p, pt: (b, 0, 0))
    gspec = pl.BlockSpec((None, Ts, LANES), lambda b, p, pt: (b, 0, 0))
    lf_pages = [pl.BlockSpec((None, None, H_F, P), lambda b, p, pt, r=r: (layer, pt[b, p * R + r], 0, 0))
                for r in range(R)]
    kv_pages = [pl.BlockSpec((None, None, H_F, DH_F, P), lambda b, p, pt, r=r: (layer, pt[b, p * R + r], 0, 0, 0))
                for r in range(R)]
    grid_spec = pltpu.PrefetchScalarGridSpec(
        num_scalar_prefetch=1, grid=(Bs, n_pages // R),
        in_specs=[tok, tok, tok, tok, gspec] + lf_pages + kv_pages + kv_pages, out_specs=[tok, gspec],
        scratch_shapes=[pltpu.VMEM((rows, C), F32), pltpu.VMEM((rows, 1), F32), pltpu.VMEM((rows, 1), F32),
                        pltpu.VMEM((rows, C), F32), pltpu.VMEM((H_F, P), F32),
                        pltpu.VMEM((P, C), F32), pltpu.VMEM((P, C), F32), pltpu.VMEM((P, LANES), F32)])
    return pl.pallas_call(
        functools.partial(_decode_fox_body, R, Ts), grid_spec=grid_spec,
        out_shape=[jax.ShapeDtypeStruct((Bs, Ts, C), F32), jax.ShapeDtypeStruct((Bs, Ts, LANES), F32)],
        compiler_params=_cparams(("arbitrary", "arbitrary")), name="decode_fox",
    )(page_table, q, k_new, v_new, z, gt, *([lfT_cache] * R), *([kT_cache] * R), *([vT_cache] * R))


def _decode_diff_body(R, Ts, lam_init, pt_ref, q_ref, kn_ref, vn_ref, z_ref, lq1, lk1, lq2, lk2, dg_ref, *refs):
    k_refs, v_refs = refs[:R], refs[R:2 * R]
    o_ref, qm, m_s, l_s, acc, maskb, kpad, vpad = refs[2 * R:]
    P = k_refs[0].shape[0]
    rows = 2 * H_D * SUBLANES
    grp = 2 * SUBLANES
    p = pl.program_id(1)

    @pl.when(p == 0)
    def _():
        lane = lax.broadcasted_iota(jnp.int32, (Ts, DV_D), 1)
        q = q_ref[...] * (DQ_D ** -0.5)
        qm[...] = jnp.zeros(qm.shape, F32)
        for h in range(H_D):
            qh = q[:, h * DV_D:(h + 1) * DV_D]
            qm[h * grp:h * grp + Ts, :] = jnp.where(lane < DQ_D, qh, 0.0)
            qm[h * grp + SUBLANES:h * grp + SUBLANES + Ts, :] = jnp.where(lane >= DQ_D, qh, 0.0)
        row_head = lax.shift_right_logical(lax.broadcasted_iota(jnp.int32, (rows, P * H_D), 0), grp.bit_length() - 1)
        col_head = lax.broadcasted_iota(jnp.int32, (rows, P * H_D), 1) & (H_D - 1)
        maskb[...] = jnp.where(row_head == col_head, 0.0, NEG_INF)
        m_s[...] = jnp.full(m_s.shape, NEG_INF, F32)
        l_s[...] = jnp.zeros(l_s.shape, F32)
        acc[...] = jnp.zeros(acc.shape, F32)

    qb = qm[...].astype(BF16)
    mb = maskb[...]
    s_parts = [_dot_nt(qb, k_refs[r][...].reshape(P * H_D, DV_D).astype(BF16)) + mb for r in range(R)]
    pr, alpha = _softmax_step(jnp.concatenate(s_parts, axis=1), m_s, l_s)
    pv = None
    for r in range(R):
        vv = v_refs[r][...].reshape(P * H_D, DV_D).astype(BF16)
        t = jnp.dot(pr[:, r * P * H_D:(r + 1) * P * H_D], vv, preferred_element_type=F32)
        pv = t if pv is None else pv + t
    acc[...] = alpha * acc[...] + pv

    @pl.when(p == pl.num_programs(1) - 1)
    def _():
        kpad[...] = jnp.zeros(kpad.shape, F32)
        kpad[0:Ts * H_D, :] = kn_ref[...]
        vpad[...] = jnp.zeros(vpad.shape, F32)
        vpad[0:Ts * H_D, :] = vn_ref[...]
        n_col = kpad.shape[0]
        r_i = lax.broadcasted_iota(jnp.int32, (rows, n_col), 0)
        j_i = lax.broadcasted_iota(jnp.int32, (rows, n_col), 1)
        key = lax.shift_right_logical(j_i, H_D.bit_length() - 1)
        row_head = lax.shift_right_logical(r_i, grp.bit_length() - 1)
        ok = ((j_i & (H_D - 1)) == row_head) & (key <= (r_i & (SUBLANES - 1))) & (key < Ts)
        s = jnp.where(ok, _dot_nt(qb, kpad[...].astype(BF16)), NEG_INF)
        pr, alpha = _softmax_step(s, m_s, l_s)
        o_full = (alpha * acc[...] + jnp.dot(pr, vpad[...].astype(BF16), preferred_element_type=F32)) / l_s[...]
        z = z_ref[...]
        lam = _lambda(lq1, lk1, lq2, lk2, lam_init)
        for h in range(H_D):
            cs = slice(h * DV_D, (h + 1) * DV_D)
            o = o_full[h * grp:h * grp + SUBLANES, :] - lam * o_full[h * grp + SUBLANES:(h + 1) * grp, :]
            o = o * lax.rsqrt(jnp.mean(o * o, axis=1, keepdims=True) + EPS) * dg_ref[...]
            o_ref[:, cs] = o[0:Ts, :] * (1.0 - lam_init) * _silu(z[:, cs])


def _decode_diff(page_table, k_cache, v_cache, layer, q, k_new, v_new, z, lam_params, lam_init, R):
    Bs, n_pages = page_table.shape
    _, Ts, C = q.shape
    P = k_cache.shape[2]
    rows = 2 * H_D * SUBLANES
    tok = pl.BlockSpec((None, Ts, C), lambda b, p, pt: (b, 0, 0))
    new = pl.BlockSpec((None, Ts * H_D, DV_D), lambda b, p, pt: (b, 0, 0))
    small = [pl.BlockSpec((1, DQ_D), lambda b, p, pt: (0, 0))] * 4 + [pl.BlockSpec((1, DV_D), lambda b, p, pt: (0, 0))]
    pages = [pl.BlockSpec((None, None, P, H_D, DV_D), lambda b, p, pt, r=r: (layer, pt[b, p * R + r], 0, 0, 0))
             for r in range(R)]
    grid_spec = pltpu.PrefetchScalarGridSpec(
        num_scalar_prefetch=1, grid=(Bs, n_pages // R),
        in_specs=[tok, new, new, tok] + small + pages + pages, out_specs=tok,
        scratch_shapes=[pltpu.VMEM((rows, DV_D), F32), pltpu.VMEM((rows, 1), F32), pltpu.VMEM((rows, 1), F32),
                        pltpu.VMEM((rows, DV_D), F32), pltpu.VMEM((rows, P * H_D), F32),
                        pltpu.VMEM((2 * SUBLANES * H_D, DV_D), F32), pltpu.VMEM((2 * SUBLANES * H_D, DV_D), F32)])
    return pl.pallas_call(
        functools.partial(_decode_diff_body, R, Ts, lam_init), grid_spec=grid_spec,
        out_shape=jax.ShapeDtypeStruct((Bs, Ts, C), F32),
        compiler_params=_cparams(("arbitrary", "arbitrary")), name="decode_diff",
    )(page_table, q, k_new, v_new, z, *[a.reshape(1, -1) for a in lam_params], *([k_cache] * R), *([v_cache] * R))


_EVEN_SRC = ((0, 5 * W_M), (5 * W_M + 2 * H_M, 5 * W_M + 2 * H_M + 4 * W_F),
             (5 * W_M, 5 * W_M + 2 * H_M), (5 * W_M + 2 * H_M + 4 * W_F, 5 * W_M + 2 * H_M + 4 * W_F + H_F))
_N_GATES = 2 * H_M + H_F


def _even_weights(w_in, b_in):
    pad = LANES - _N_GATES
    w = jnp.concatenate([w_in[:, a:b] for a, b in _EVEN_SRC] + [jnp.zeros((w_in.shape[0], pad), w_in.dtype)], axis=1)
    b = jnp.concatenate([b_in[a:b] for a, b in _EVEN_SRC] + [jnp.zeros((pad,), b_in.dtype)])
    return w.astype(BF16), b


def _even_plan(attn_dtype, q_scale):
    kv = (F32,) if attn_dtype == F32 else (F32, attn_dtype)
    return ((0, 3 * W_M, (F32,), 1.0),
            (3 * W_M, 2 * W_M, (F32,), 1.0),
            (5 * W_M, W_F, (attn_dtype,), q_scale),
            (5 * W_M + W_F, W_F, kv, 1.0),
            (5 * W_M + 2 * W_F, W_F, kv, 1.0),
            (5 * W_M + 3 * W_F, W_F, (F32,), 1.0),
            (5 * W_M + 4 * W_F, LANES, (F32,), 1.0))


def _odd_plan(attn_dtype, q_scale):
    kv = (F32,) if attn_dtype == F32 else (F32, attn_dtype)
    return ((0, W_D, (attn_dtype,), q_scale), (W_D, W_D, kv, 1.0), (2 * W_D, W_D, kv, 1.0),
            (3 * W_D, W_D, (F32,), 1.0))


def _tile(n, pref):
    t = min(n, pref)
    while n % t:
        t //= 2
    return t


def kernel(x_prompt, x_sample, cache_fox_k, cache_fox_v, cache_fox_logf, cache_diff_k, cache_diff_v, state_mlstm_c, state_mlstm_n, state_mlstm_m, state_mlstm_conv, page_table, norm_g, final_norm_g, w_in_even, b_in_even, conv_w, conv_b, mlstm_norm_g, w_out_even, w_in_odd, lambda_q1, lambda_k1, lambda_q2, lambda_k2, diff_norm_g, w_out_odd):
    B, T, D = x_prompt.shape
    Bs, Ts, _ = x_sample.shape
    depth = norm_g.shape[0]
    n_pool, P = cache_fox_k.shape[1], cache_fox_k.shape[2]
    n_pages = page_table.shape[1]
    Mp, Ms = B * T, Bs * Ts
    tm_p, tm_s = _tile(Mp, 256), _tile(Ms, 256)
    tq = _tile(T, 512)
    q_scale = LOG2E * DH_F ** -0.5
    R = _tile(n_pages, 8)
    fox_kT_cache = jnp.transpose(cache_fox_k, (0, 1, 3, 4, 2))
    fox_vT_cache = jnp.transpose(cache_fox_v, (0, 1, 3, 4, 2))
    fox_lfT_cache = jnp.transpose(cache_fox_logf, (0, 1, 3, 2))

    hp = x_prompt.reshape(Mp, D)
    hs = x_sample.reshape(Ms, D)
    names = ("fox_k", "fox_v", "fox_logf", "diff_k", "diff_v", "mlstm_c", "mlstm_n", "mlstm_m", "mlstm_conv")
    outs_p = {nm: [] for nm in names}
    outs_s = {nm: [] for nm in names}

    for l in range(depth):
        i = l // 2
        last = l == depth - 1
        fg = final_norm_g if last else None
        if l % 2 == 0:
            w, b = _even_weights(w_in_even[i], b_in_even[i])
            wo = w_out_even[i].astype(BF16)
            wo_m, wo_f = wo[:W_M], wo[W_M:]
            qkv, oz, qf, kf, kfb, vf, vfb, zf, gt = _proj(hp, norm_g[l], w, b, _even_plan(BF16, q_scale), tm_p)
            hm, c_p, n_p, m_p, cv_p = _mlstm(qkv.reshape(B, T, -1), oz.reshape(B, T, -1), gt.reshape(B, T, -1),
                                             conv_w[i], conv_b[i], mlstm_norm_g[i], None, BF16)
            lff, ft = _fcum(gt.reshape(B, T, LANES))
            hf = _attn("fox", qf.reshape(B, T, W_F), kfb.reshape(B, T, W_F), vfb.reshape(B, T, W_F),
                       zf.reshape(B, T, W_F), (_fox_bias_layout(ft, tq),), 0.0, tq)
            hp = _outproj(hp, [hm.reshape(Mp, W_M), hf.reshape(Mp, W_F)], [wo_m, wo_f], fg, tm_p)
            outs_p["fox_k"].append(kf.reshape(B, T, H_F, DH_F))
            outs_p["fox_v"].append(vf.reshape(B, T, H_F, DH_F))
            outs_p["fox_logf"].append(lff[:, :, 2 * H_M:2 * H_M + H_F])
            outs_p["mlstm_c"].append(c_p)
            outs_p["mlstm_n"].append(n_p)
            outs_p["mlstm_m"].append(m_p[:, :, 0])
            outs_p["mlstm_conv"].append(cv_p)
            qkv, oz, qf, kf, vf, zf, gt = _proj(hs, norm_g[l], w, b, _even_plan(F32, 1.0), tm_s)
            state = (state_mlstm_c[i], state_mlstm_n[i],
                     jnp.broadcast_to(state_mlstm_m[i][:, :, None], (Bs, H_M, LANES)), state_mlstm_conv[i])
            hm, c_s, n_s, m_s, cv_s = _mlstm(qkv.reshape(Bs, Ts, -1), oz.reshape(Bs, Ts, -1), gt.reshape(Bs, Ts, -1),
                                             conv_w[i], conv_b[i], mlstm_norm_g[i], state, F32)
            hf, lffn = _decode_fox(page_table, fox_kT_cache, fox_vT_cache, fox_lfT_cache, i, qf.reshape(Bs, Ts, W_F),
                                   kf.reshape(Bs, Ts, W_F), vf.reshape(Bs, Ts, W_F), zf.reshape(Bs, Ts, W_F),
                                   gt.reshape(Bs, Ts, LANES), R)
            hs = _outproj(hs, [hm.reshape(Ms, W_M), hf.reshape(Ms, W_F)], [wo_m, wo_f], fg, tm_s)
            outs_s["fox_k"].append(kf.reshape(Bs, Ts, H_F, DH_F))
            outs_s["fox_v"].append(vf.reshape(Bs, Ts, H_F, DH_F))
            outs_s["fox_logf"].append(lffn[:, :, 2 * H_M:2 * H_M + H_F])
            outs_s["mlstm_c"].append(c_s)
            outs_s["mlstm_n"].append(n_s)
            outs_s["mlstm_m"].append(m_s[:, :, 0])
            outs_s["mlstm_conv"].append(cv_s)
        else:
            lam_init = 0.8 - 0.6 * float(np.exp(-0.3 * l))
            w = w_in_odd[i].astype(BF16)
            wo = w_out_odd[i].astype(BF16)
            lam_p = (lambda_q1[i], lambda_k1[i], lambda_q2[i], lambda_k2[i], diff_norm_g[i])
            q, k, kb, v, vb, z = _proj(hp, norm_g[l], w, None, _odd_plan(BF16, q_scale), tm_p)
            o = _attn("diff", q.reshape(B, T, W_D), kb.reshape(B, T, W_D), vb.reshape(B, T, W_D),
                      z.reshape(B, T, W_D), lam_p, lam_init, tq)
            hp = _outproj(hp, [o.reshape(Mp, W_D)], [wo], fg, tm_p)
            outs_p["diff_k"].append(k.reshape(B, T, H_D, 2 * DQ_D))
            outs_p["diff_v"].append(v.reshape(B, T, H_D, DV_D))
            q, k, v, z = _proj(hs, norm_g[l], w, None, _odd_plan(F32, 1.0), tm_s)
            o = _decode_diff(page_table, cache_diff_k, cache_diff_v, i, q.reshape(Bs, Ts, W_D),
                             k.reshape(Bs, Ts * H_D, DV_D), v.reshape(Bs, Ts * H_D, DV_D), z.reshape(Bs, Ts, W_D),
                             lam_p, lam_init, R)
            hs = _outproj(hs, [o.reshape(Ms, W_D)], [wo], fg, tm_s)
            outs_s["diff_k"].append(k.reshape(Bs, Ts, H_D, 2 * DQ_D))
            outs_s["diff_v"].append(v.reshape(Bs, Ts, H_D, DV_D))

    return (hp.reshape(B, T, D), hs.reshape(Bs, Ts, D),
            *[jnp.stack(outs_p[nm]) for nm in names], *[jnp.stack(outs_s[nm]) for nm in names])
```

```python
import functools

import numpy as np
import jax
import jax.numpy as jnp
from jax import lax
from jax.experimental import pallas as pl
from jax.experimental.pallas import tpu as pltpu

F32 = jnp.float32
BF16 = jnp.bfloat16
NEG_INF = float("-inf")

EPS = 1e-6
LOG2E = 1.4426950408889634
H_M, DH_M = 4, 128
H_F, DH_F = 8, 64
H_D, DV_D, DQ_D = 8, 128, 64
CONV_W = 4
M_CHUNK = 128
LANES = 128
SUBLANES = 8
VMEM_LIMIT = 52 * 1024 * 1024

W_M = H_M * DH_M
W_F = H_F * DH_F
W_D = H_D * DV_D


def _cparams(sem):
    return pltpu.CompilerParams(dimension_semantics=sem, vmem_limit_bytes=VMEM_LIMIT)


def _sigmoid(x):
    return 1.0 / (1.0 + jnp.exp(-x))


def _silu(x):
    return x * _sigmoid(x)


def _log_sigmoid(x):
    return jnp.minimum(x, 0.0) - jnp.log(1.0 + jnp.exp(-jnp.abs(x)))


def _dot_nt(a, b):
    return lax.dot_general(a, b, (((1,), (1,)), ((), ())), preferred_element_type=F32)


TIME_MINOR = "time_minor"
HEAD_MAJOR = "head_major"


def _proj_body(plan, has_bias, n_prev, x_ref, g_ref, w_ref, *rest):
    if has_bias:
        b_ref, *rest = rest
    out_refs = rest[n_prev:]
    x = x_ref[...]
    xn = x * lax.rsqrt(jnp.mean(x * x, axis=-1, keepdims=True) + EPS) * g_ref[...]
    xn = xn.astype(BF16)
    k = 0
    for c0, width, kinds, scale in plan:
        y = jnp.dot(xn, w_ref[:, c0:c0 + width], preferred_element_type=F32)
        if has_bias:
            y = y + b_ref[:, c0:c0 + width]
        if scale != 1.0:
            y = y * scale
        for kind in kinds:
            if kind == TIME_MINOR:
                out_refs[k][...] = y.T
            elif kind == HEAD_MAJOR:
                out_refs[k][...] = y.reshape(out_refs[k].shape)
            else:
                out_refs[k][...] = y.astype(kind)
            k += 1


def _proj(x, g, w, b, plan, tm, stack=None):
    M, D = x.shape
    N = w.shape[1]
    has_bias = b is not None
    layer, n_layers, T, prev = stack if stack is not None else (0, 1, M, None)
    out_shape, out_specs, stacked = [], [], []
    for _, width, kinds, _ in plan:
        for kind in kinds:
            if kind == TIME_MINOR:
                nt = T // tm
                out_shape.append(jax.ShapeDtypeStruct((n_layers, M // T, width, T), F32))
                out_specs.append(pl.BlockSpec((None, None, width, tm), lambda i: (layer, i // nt, 0, i % nt)))
                stacked.append(len(out_shape) - 1)
            elif kind == HEAD_MAJOR:
                out_shape.append(jax.ShapeDtypeStruct((n_layers, M, H_D, width // H_D), F32))
                out_specs.append(pl.BlockSpec((None, tm, H_D, width // H_D), lambda i: (layer, i, 0, 0)))
                stacked.append(len(out_shape) - 1)
            else:
                out_shape.append(jax.ShapeDtypeStruct((M, width), kind))
                out_specs.append(pl.BlockSpec((tm, width), lambda i: (i, 0)))
    in_specs = [pl.BlockSpec((tm, D), lambda i: (i, 0)),
                pl.BlockSpec((1, D), lambda i: (0, 0)),
                pl.BlockSpec((D, N), lambda i: (0, 0))]
    args = [x, g.reshape(1, D), w]
    if has_bias:
        in_specs.append(pl.BlockSpec((1, N), lambda i: (0, 0)))
        args.append(b.reshape(1, N))
    aliases = {}
    if prev is not None:
        for a, o in zip(prev, stacked):
            aliases[len(args)] = o
            in_specs.append(pl.BlockSpec(memory_space=pl.ANY))
            args.append(a)
    return pl.pallas_call(
        functools.partial(_proj_body, plan, has_bias, len(aliases)),
        grid=(M // tm,), in_specs=in_specs, out_specs=out_specs, out_shape=out_shape,
        input_output_aliases=aliases, compiler_params=_cparams(("arbitrary",)), name="proj",
    )(*args)


def _mlstm_body(Lv, has_state, qkv_ref, oz_ref, gt_ref, cw_ref, cb_ref, mg_ref, *rest):
    L = M_CHUNK
    if has_state:
        c0_ref, n0_ref, m0_ref, cv0_ref, *rest = rest
    h_ref, cN_ref, nN_ref, mN_ref, cvN_ref, ubuf, *pads = rest
    c = pl.program_id(1)

    @pl.when(c == 0)
    def _():
        ubuf[...] = jnp.zeros(ubuf.shape, F32)
        if has_state:
            ubuf[SUBLANES - (CONV_W - 1):SUBLANES, :] = cv0_ref[...]
            cN_ref[...] = c0_ref[...]
            nN_ref[...] = n0_ref[...]
            mN_ref[...] = m0_ref[...]
        else:
            cN_ref[...] = jnp.zeros(cN_ref.shape, F32)
            nN_ref[...] = jnp.zeros(nN_ref.shape, F32)
            mN_ref[...] = jnp.zeros(mN_ref.shape, F32)

    ubuf[SUBLANES:SUBLANES + Lv, :] = qkv_ref[:, 0:2 * W_M]
    acc = None
    for j in range(CONV_W):
        r0 = SUBLANES - (CONV_W - 1) + j
        term = ubuf[r0:r0 + L, :] * cw_ref[j:j + 1, :]
        acc = term if acc is None else acc + term
    y = acc + cb_ref[...]
    tail = ubuf[SUBLANES + Lv - (CONV_W - 1):SUBLANES + Lv, :]
    ubuf[SUBLANES - (CONV_W - 1):SUBLANES, :] = tail
    cvN_ref[...] = tail
    qk = _silu(y)

    if Lv == L:
        G = gt_ref[...]
        V = qkv_ref[:, 2 * W_M:3 * W_M]
    else:
        gbuf, vbuf = pads
        gbuf[...] = jnp.zeros(gbuf.shape, F32)
        gbuf[0:Lv, :] = gt_ref[...]
        vbuf[...] = jnp.zeros(vbuf.shape, F32)
        vbuf[0:Lv, :] = qkv_ref[:, 2 * W_M:3 * W_M]
        G = gbuf[...]
        V = vbuf[...]
    LF = _log_sigmoid(G)
    GT = G.T
    LFT = LF.T
    r_i = lax.broadcasted_iota(jnp.int32, (L, L), 0)
    c_i = lax.broadcasted_iota(jnp.int32, (L, L), 1)
    tril = r_i >= c_i
    triu = r_i <= c_i
    rowv = lax.broadcasted_iota(jnp.int32, (L, 1), 0) < Lv
    colv = lax.broadcasted_iota(jnp.int32, (1, L), 1) < Lv

    for h in range(H_M):
        hs = slice(h * DH_M, (h + 1) * DH_M)
        q = qk[:, h * DH_M:(h + 1) * DH_M]
        k = qk[:, W_M + h * DH_M:W_M + (h + 1) * DH_M] * (DH_M ** -0.5)
        vb = V[:, hs].astype(BF16)
        ig_row = GT[h:h + 1, :]
        lf_row = LFT[H_M + h:H_M + h + 1, :]
        ig_col = G[:, h:h + 1]
        lf_col = LF[:, H_M + h:H_M + h + 1]
        if Lv < L:
            lf_row = jnp.where(colv, lf_row, 0.0)
            lf_col = jnp.where(rowv, lf_col, 0.0)
            ig_row = jnp.where(colv, ig_row, NEG_INF)
            ig_col = jnp.where(rowv, ig_col, NEG_INF)
        F_col = jnp.sum(jnp.where(tril, lf_row, 0.0), axis=1, keepdims=True)
        F_row = jnp.sum(jnp.where(triu, lf_col, 0.0), axis=0, keepdims=True)
        Dm = jnp.where(tril, F_col - F_row + ig_row, NEG_INF)
        m_prev = mN_ref[h:h + 1, 0:1]
        inter = F_col + m_prev
        m_t = jnp.maximum(jnp.max(Dm, axis=1, keepdims=True), inter)
        qb = q.astype(BF16)
        S = jnp.exp(Dm - m_t) * _dot_nt(qb, k.astype(BF16))
        w_inter = jnp.exp(inter - m_t)
        C = cN_ref[h]
        n_row = nN_ref[h:h + 1, :]
        num = (jnp.dot(S.astype(BF16), vb, preferred_element_type=F32)
               + w_inter * jnp.dot(qb, C.astype(BF16), preferred_element_type=F32))
        den = jnp.sum(S, axis=1, keepdims=True) + w_inter * jnp.sum(q * n_row, axis=1, keepdims=True)
        hh = num / jnp.maximum(jnp.abs(den), jnp.exp(-m_t))

        m_new = m_t[Lv - 1:Lv, :]
        F_last = F_col[Lv - 1:Lv, :]
        w_s = jnp.exp(F_last - F_col + ig_col - m_new)
        decay = jnp.exp(F_last + m_prev - m_new)
        kw = k * w_s
        cN_ref[h] = decay * C + jnp.dot(kw.T.astype(BF16), vb, preferred_element_type=F32)
        nN_ref[h:h + 1, :] = decay * n_row + jnp.sum(kw, axis=0, keepdims=True)
        mN_ref[h:h + 1, :] = jnp.broadcast_to(m_new, (1, LANES))

        hn = hh * lax.rsqrt(jnp.mean(hh * hh, axis=1, keepdims=True) + EPS) * mg_ref[:, hs]
        hn = hn[0:Lv, :]
        o = oz_ref[:, h * DH_M:(h + 1) * DH_M]
        z = oz_ref[:, W_M + h * DH_M:W_M + (h + 1) * DH_M]
        h_ref[:, hs] = (hn * _sigmoid(o) * _silu(z)).astype(h_ref.dtype)


def _mlstm(qkv, oz, gt, conv_w, conv_b, mnorm_g, state, out_dtype):
    B, T, _ = qkv.shape
    Lv = min(T, M_CHUNK)
    nc = T // Lv
    has_state = state is not None
    row = lambda b, c: (b, c, 0)
    fix2 = lambda b, c: (0, 0)
    per_b3 = lambda b, c: (b, 0, 0)
    per_b4 = lambda b, c: (b, 0, 0, 0)
    in_specs = [pl.BlockSpec((None, Lv, 3 * W_M), row),
                pl.BlockSpec((None, Lv, 2 * W_M), row),
                pl.BlockSpec((None, Lv, LANES), row),
                pl.BlockSpec((CONV_W, 2 * W_M), fix2),
                pl.BlockSpec((1, 2 * W_M), fix2),
                pl.BlockSpec((1, W_M), fix2)]
    args = [qkv, oz, gt, conv_w, conv_b.reshape(1, -1), mnorm_g.reshape(1, -1)]
    state_specs = [pl.BlockSpec((None, H_M, DH_M, DH_M), per_b4),
                   pl.BlockSpec((None, H_M, DH_M), per_b3),
                   pl.BlockSpec((None, H_M, LANES), per_b3),
                   pl.BlockSpec((None, CONV_W - 1, 2 * W_M), per_b3)]
    if has_state:
        in_specs += state_specs
        args += list(state)
    out_shape = [jax.ShapeDtypeStruct((B, T, W_M), out_dtype),
                 jax.ShapeDtypeStruct((B, H_M, DH_M, DH_M), F32),
                 jax.ShapeDtypeStruct((B, H_M, DH_M), F32),
                 jax.ShapeDtypeStruct((B, H_M, LANES), F32),
                 jax.ShapeDtypeStruct((B, CONV_W - 1, 2 * W_M), F32)]
    out_specs = [pl.BlockSpec((None, Lv, W_M), row)] + state_specs
    scratch = [pltpu.VMEM((SUBLANES + M_CHUNK, 2 * W_M), F32)]
    if Lv < M_CHUNK:
        scratch += [pltpu.VMEM((M_CHUNK, LANES), F32), pltpu.VMEM((M_CHUNK, W_M), F32)]
    return pl.pallas_call(
        functools.partial(_mlstm_body, Lv, has_state),
        grid=(B, nc), in_specs=in_specs, out_specs=out_specs, out_shape=out_shape,
        scratch_shapes=scratch, compiler_params=_cparams(("arbitrary", "arbitrary")), name="mlstm",
    )(*args)


def _fcum_body(n_chunks, g_ref, lff_ref, ft_ref):
    L = LANES
    triu = lax.broadcasted_iota(jnp.int32, (L, L), 0) <= lax.broadcasted_iota(jnp.int32, (L, L), 1)

    def chunk(c, carry):
        r0 = pl.multiple_of(c * L, L)
        LF = _log_sigmoid(g_ref[pl.ds(r0, L), :])
        lff_ref[pl.ds(r0, L), :] = LF
        new = []
        for h in range(H_F):
            col = LF[:, H_F + h:H_F + h + 1]
            fr = jnp.sum(jnp.where(triu, col, 0.0), axis=0, keepdims=True) + carry[h]
            ft_ref[c, h:h + 1, :] = fr * LOG2E
            new.append(fr[:, L - 1:L])
        return tuple(new)

    lax.fori_loop(0, n_chunks, chunk, tuple(jnp.zeros((1, 1), F32) for _ in range(H_F)))


def _fcum(gt):
    B, T, _ = gt.shape
    nc = T // LANES
    return pl.pallas_call(
        functools.partial(_fcum_body, nc),
        grid=(B,),
        in_specs=[pl.BlockSpec((None, T, LANES), lambda b: (b, 0, 0))],
        out_specs=[pl.BlockSpec((None, T, LANES), lambda b: (b, 0, 0)),
                   pl.BlockSpec((None, nc, H_F, LANES), lambda b: (b, 0, 0, 0))],
        out_shape=[jax.ShapeDtypeStruct((B, T, LANES), F32), jax.ShapeDtypeStruct((B, nc, H_F, LANES), F32)],
        compiler_params=_cparams(("arbitrary",)), name="fcum",
    )(gt)


def _fox_bias_layout(ft, tq):
    B, nc, _, L = ft.shape
    per = tq // L
    x = ft.reshape(B, nc // per, per, H_F // 2, 2, L)
    return x.transpose(0, 3, 1, 4, 2, 5).reshape(B, H_F // 2, nc // per, 2, tq)


def _lambda(lq1_ref, lk1_ref, lq2_ref, lk2_ref, lam_init):
    return (jnp.exp(jnp.sum(lq1_ref[...] * lk1_ref[...], axis=1, keepdims=True))
            - jnp.exp(jnp.sum(lq2_ref[...] * lk2_ref[...], axis=1, keepdims=True)) + lam_init)


def _attn_body(mode, tq, lam_init, q_ref, k_ref, v_ref, z_ref, *rest):
    if mode == "fox":
        ft_ref, o_ref, q2, p_scr, m_s, acc = rest
    else:
        lq1, lk1, lq2, lk2, dg_ref, o_ref, q2, p_scr, m_s, acc = rest
    qi = pl.program_id(2)
    rows = 2 * tq
    lane_row = lax.broadcasted_iota(jnp.int32, (1, LANES), 1)
    q = q_ref[...]
    q2[0:tq, :] = q * jnp.where(lane_row < DH_F, 1.0, 0.0).astype(BF16)
    q2[tq:rows, :] = q * jnp.where(lane_row >= DH_F, 1.0, 0.0).astype(BF16)
    m_s[...] = jnp.full(m_s.shape, NEG_INF, F32)
    acc[...] = jnp.zeros(acc.shape, F32)
    p_scr[...] = jnp.zeros(p_scr.shape, BF16)
    ones = jnp.ones((tq, LANES), BF16)

    def scores(j):
        k0 = pl.multiple_of(j * tq, tq)
        return _dot_nt(q2[...], k_ref[pl.ds(k0, tq), :])

    def pv(j):
        k0 = pl.multiple_of(j * tq, tq)
        vext = jnp.concatenate([v_ref[pl.ds(k0, tq), :], ones], axis=1)
        return jnp.dot(p_scr[...], vext, preferred_element_type=F32)

    def softmax(s, j, masked):
        if mode == "fox":
            s = jnp.concatenate([s[0:tq] - ft_ref[j, 0:1, :], s[tq:rows] - ft_ref[j, 1:2, :]], axis=0)
        if masked:
            r_i = lax.broadcasted_iota(jnp.int32, (rows, tq), 0) & (tq - 1)
            c_i = lax.broadcasted_iota(jnp.int32, (rows, tq), 1)
            s = jnp.where(r_i >= c_i, s, NEG_INF)
        m_old = m_s[...]
        m_new = jnp.maximum(m_old, jnp.max(s, axis=1, keepdims=True))
        p = jnp.exp2(s - jnp.concatenate([m_new] * (tq // LANES), axis=1)).astype(BF16)
        alpha = jnp.exp2(m_old - m_new)
        m_s[...] = m_new
        return p, jnp.concatenate([alpha, alpha], axis=1)

    def body(j, c):
        pend = pv(jnp.maximum(j - 1, 0))
        p, alpha = softmax(scores(j), j, False)
        acc[...] = alpha * (acc[...] + pend)
        p_scr[...] = p
        return c

    lax.fori_loop(0, qi, body, 0)
    pend = pv(jnp.maximum(qi - 1, 0))
    p, alpha = softmax(scores(qi), qi, True)
    a = alpha * (acc[...] + pend)
    p_scr[...] = p
    a = a + pv(qi)
    o0 = a[0:tq, 0:LANES] / a[0:tq, LANES:2 * LANES]
    o1 = a[tq:rows, 0:LANES] / a[tq:rows, LANES:2 * LANES]
    z = z_ref[...]
    if mode == "fox":
        lane = lax.broadcasted_iota(jnp.int32, (tq, LANES), 1)
        o = jnp.where(lane < DH_F, o0, o1)
        o_ref[...] = (o * _silu(z)).astype(o_ref.dtype)
    else:
        lam = _lambda(lq1, lk1, lq2, lk2, lam_init)
        o = o0 - lam * o1
        o = o * lax.rsqrt(jnp.mean(o * o, axis=1, keepdims=True) + EPS) * dg_ref[...]
        o_ref[...] = (o * (1.0 - lam_init) * _silu(z)).astype(o_ref.dtype)


def _attn(mode, q, k, v, z, extra, lam_init, tq):
    B, T, C = q.shape
    G = C // LANES
    scratch = [pltpu.VMEM((2 * tq, LANES), BF16), pltpu.VMEM((2 * tq, tq), BF16),
               pltpu.VMEM((2 * tq, LANES), F32), pltpu.VMEM((2 * tq, 2 * LANES), F32)]
    qspec = pl.BlockSpec((None, tq, LANES), lambda b, g, i: (b, i, g))
    kspec = pl.BlockSpec((None, T, LANES), lambda b, g, i: (b, 0, g))
    in_specs = [qspec, kspec, kspec, qspec]
    if mode == "fox":
        (ft,) = extra
        in_specs.append(pl.BlockSpec((None, None, T // tq, 2, tq), lambda b, g, i: (b, g, 0, 0, 0)))
        args = [q, k, v, z, ft]
    else:
        lq1, lk1, lq2, lk2, dg = extra
        in_specs += [pl.BlockSpec((1, DQ_D), lambda b, g, i: (0, 0))] * 4
        in_specs.append(pl.BlockSpec((1, DV_D), lambda b, g, i: (0, 0)))
        args = [q, k, v, z] + [a.reshape(1, -1) for a in (lq1, lk1, lq2, lk2, dg)]
    return pl.pallas_call(
        functools.partial(_attn_body, mode, tq, lam_init),
        grid=(B, G, T // tq), in_specs=in_specs, out_specs=qspec,
        out_shape=jax.ShapeDtypeStruct((B, T, C), BF16), scratch_shapes=scratch,
        compiler_params=_cparams(("arbitrary", "arbitrary", "arbitrary")), name="attn_" + mode,
    )(*args)


def _outproj_body(n_in, final, x_ref, *rest):
    a_refs, w_refs = rest[:n_in], rest[n_in:2 * n_in]
    rest = rest[2 * n_in:]
    acc = None
    for a_ref, w_ref in zip(a_refs, w_refs):
        t = jnp.dot(a_ref[...].astype(BF16), w_ref[...], preferred_element_type=F32)
        acc = t if acc is None else acc + t
    y = x_ref[...] + acc
    if final:
        g_ref, o_ref = rest
        y = y * lax.rsqrt(jnp.mean(y * y, axis=-1, keepdims=True) + EPS) * g_ref[...]
    else:
        (o_ref,) = rest
    o_ref[...] = y


def _outproj(x, acts, ws, final_g, tm):
    M, D = x.shape
    n_in = len(acts)
    final = final_g is not None
    in_specs = [pl.BlockSpec((tm, D), lambda i: (i, 0))]
    in_specs += [pl.BlockSpec((tm, a.shape[1]), lambda i: (i, 0)) for a in acts]
    in_specs += [pl.BlockSpec(w.shape, lambda i: (0, 0)) for w in ws]
    args = [x, *acts, *ws]
    if final:
        in_specs.append(pl.BlockSpec((1, D), lambda i: (0, 0)))
        args.append(final_g.reshape(1, D))
    return pl.pallas_call(
        functools.partial(_outproj_body, n_in, final),
        grid=(M // tm,), in_specs=in_specs, out_specs=pl.BlockSpec((tm, D), lambda i: (i, 0)),
        out_shape=jax.ShapeDtypeStruct((M, D), F32),
        compiler_params=_cparams(("arbitrary",)), name="outproj",
    )(*args)


def _softmax_step(s, m_s, l_s):
    m_old = m_s[...]
    m_new = jnp.maximum(m_old, jnp.max(s, axis=1, keepdims=True))
    pr = jnp.exp(s - m_new)
    alpha = jnp.exp(m_old - m_new)
    l_s[...] = alpha * l_s[...] + jnp.sum(pr, axis=1, keepdims=True)
    m_s[...] = m_new
    return pr.astype(BF16), alpha


def _decode_fox_body(R, Ts, pt_ref, q_ref, kn_ref, vn_ref, z_ref, g_ref, *refs):
    lf_refs, k_refs, v_refs = refs[:R], refs[R:2 * R], refs[2 * R:3 * R]
    o_ref, lffn_ref, qm, m_s, l_s, acc, carry, kpad, vpad, gpad = refs[3 * R:]
    P = LANES
    rows = H_F * SUBLANES
    p = pl.program_id(1)

    @pl.when(p == 0)
    def _():
        lane = lax.broadcasted_iota(jnp.int32, (Ts, W_F), 1)
        q = q_ref[...] * (DH_F ** -0.5)
        qm[...] = jnp.zeros(qm.shape, F32)
        for h in range(H_F):
            qm[h * SUBLANES:h * SUBLANES + Ts, :] = jnp.where((lane >= h * DH_F) & (lane < (h + 1) * DH_F), q, 0.0)
        m_s[...] = jnp.full(m_s.shape, NEG_INF, F32)
        l_s[...] = jnp.zeros(l_s.shape, F32)
        acc[...] = jnp.zeros(acc.shape, F32)
        carry[...] = jnp.zeros(carry.shape, F32)

    qb = qm[...].astype(BF16)

    def bias_rows(f):
        return jnp.concatenate([jnp.broadcast_to(f[h:h + 1, :], (SUBLANES, P)) for h in range(H_F)], axis=0)

    lf = jnp.concatenate([lf_refs[r][...] for r in range(R)], axis=0)
    tri = (lax.broadcasted_iota(jnp.int32, (P, P), 0) <= lax.broadcasted_iota(jnp.int32, (P, P), 1))
    tri = jnp.where(tri, 1.0, 0.0).astype(BF16)
    hi = lf.astype(BF16)
    rem = lf - hi.astype(F32)
    mid = rem.astype(BF16)
    lo = (rem - mid.astype(F32)).astype(BF16)
    f = (jnp.dot(hi, tri, preferred_element_type=F32) + jnp.dot(mid, tri, preferred_element_type=F32)
         + jnp.dot(lo, tri, preferred_element_type=F32))
    totals = [jnp.broadcast_to(f[r * H_F:(r + 1) * H_F, P - 1:P], (H_F, P)) for r in range(R)]
    run = carry[...]
    s_parts = []
    for r in range(R):
        kT = k_refs[r][...].reshape(W_F, P).astype(BF16)
        s_parts.append(jnp.dot(qb, kT, preferred_element_type=F32) - bias_rows(f[r * H_F:(r + 1) * H_F, :] + run))
        run = run + totals[r]
    carry[...] = run
    pr, alpha = _softmax_step(jnp.concatenate(s_parts, axis=1), m_s, l_s)
    pv = None
    for r in range(R):
        vT = v_refs[r][...].reshape(W_F, P).astype(BF16)
        t = _dot_nt(pr[:, r * P:(r + 1) * P], vT)
        pv = t if pv is None else pv + t
    acc[...] = alpha * acc[...] + pv

    @pl.when(p == pl.num_programs(1) - 1)
    def _():
        lf = _log_sigmoid(g_ref[...])
        lffn_ref[...] = lf
        gpad[...] = jnp.zeros(gpad.shape, F32)
        gpad[0:Ts, :] = lf
        lfp = gpad[...]
        triu = lax.broadcasted_iota(jnp.int32, (P, P), 0) <= lax.broadcasted_iota(jnp.int32, (P, P), 1)
        fn = [jnp.sum(jnp.where(triu, lfp[:, 2 * H_M + h:2 * H_M + h + 1], 0.0), axis=0, keepdims=True)
              + carry[h:h + 1, :] for h in range(H_F)]
        bias = jnp.concatenate([jnp.broadcast_to(f, (SUBLANES, P)) for f in fn], axis=0)
        kpad[...] = jnp.zeros(kpad.shape, F32)
        kpad[0:Ts, :] = kn_ref[...]
        vpad[...] = jnp.zeros(vpad.shape, F32)
        vpad[0:Ts, :] = vn_ref[...]
        t_i = lax.broadcasted_iota(jnp.int32, (rows, P), 0) & (SUBLANES - 1)
        j_i = lax.broadcasted_iota(jnp.int32, (rows, P), 1)
        s = _dot_nt(qb, kpad[...].astype(BF16)) - bias
        s = jnp.where((j_i <= t_i) & (j_i < Ts), s, NEG_INF)
        pr, alpha = _softmax_step(s, m_s, l_s)
        o_full = (alpha * acc[...] + jnp.dot(pr, vpad[...].astype(BF16), preferred_element_type=F32)) / l_s[...]
        z = z_ref[...]
        lane = lax.broadcasted_iota(jnp.int32, (SUBLANES, LANES), 1)
        for j in range(W_F // LANES):
            cs = slice(j * LANES, (j + 1) * LANES)
            o = jnp.where(lane < DH_F, o_full[(2 * j) * SUBLANES:(2 * j + 1) * SUBLANES, cs],
                          o_full[(2 * j + 1) * SUBLANES:(2 * j + 2) * SUBLANES, cs])
            o_ref[:, cs] = o[0:Ts, :] * _silu(z[:, cs])


def _decode_fox(page_table, kT_cache, vT_cache, lfT_cache, layer, q, k_new, v_new, z, gt, R):
    Bs, n_pages = page_table.shape
    _, Ts, C = q.shape
    P = kT_cache.shape[-1]
    rows = H_F * SUBLANES
    tok = pl.BlockSpec((None, Ts, C), lambda b, p, pt: (b, 0, 0))
    gspec = pl.BlockSpec((None, Ts, LANES), lambda b, p, pt: (b, 0, 0))
    lf_pages = [pl.BlockSpec((None, None, H_F, P), lambda b, p, pt, r=r: (layer, pt[b, p * R + r], 0, 0))
                for r in range(R)]
    kv_pages = [pl.BlockSpec((None, None, H_F, DH_F, P), lambda b, p, pt, r=r: (layer, pt[b, p * R + r], 0, 0, 0))
                for r in range(R)]
    grid_spec = pltpu.PrefetchScalarGridSpec(
        num_scalar_prefetch=1, grid=(Bs, n_pages // R),
        in_specs=[tok, tok, tok, tok, gspec] + lf_pages + kv_pages + kv_pages, out_specs=[tok, gspec],
        scratch_shapes=[pltpu.VMEM((rows, C), F32), pltpu.VMEM((rows, 1), F32), pltpu.VMEM((rows, 1), F32),
                        pltpu.VMEM((rows, C), F32), pltpu.VMEM((H_F, P), F32),
                        pltpu.VMEM((P, C), F32), pltpu.VMEM((P, C), F32), pltpu.VMEM((P, LANES), F32)])
    return pl.pallas_call(
        functools.partial(_decode_fox_body, R, Ts), grid_spec=grid_spec,
        out_shape=[jax.ShapeDtypeStruct((Bs, Ts, C), F32), jax.ShapeDtypeStruct((Bs, Ts, LANES), F32)],
        compiler_params=_cparams(("arbitrary", "arbitrary")), name="decode_fox",
    )(page_table, q, k_new, v_new, z, gt, *([lfT_cache] * R), *([kT_cache] * R), *([vT_cache] * R))


def _decode_diff_body(R, Ts, lam_init, pt_ref, q_ref, kn_ref, vn_ref, z_ref, lq1, lk1, lq2, lk2, dg_ref, *refs):
    k_refs, v_refs = refs[:R], refs[R:2 * R]
    o_ref, qm, m_s, l_s, acc, maskb, kpad, vpad = refs[2 * R:]
    P = k_refs[0].shape[0]
    rows = 2 * H_D * SUBLANES
    grp = 2 * SUBLANES
    p = pl.program_id(1)

    @pl.when(p == 0)
    def _():
        lane = lax.broadcasted_iota(jnp.int32, (Ts, DV_D), 1)
        q = q_ref[...] * (DQ_D ** -0.5)
        qm[...] = jnp.zeros(qm.shape, F32)
        for h in range(H_D):
            qh = q[:, h * DV_D:(h + 1) * DV_D]
            qm[h * grp:h * grp + Ts, :] = jnp.where(lane < DQ_D, qh, 0.0)
            qm[h * grp + SUBLANES:h * grp + SUBLANES + Ts, :] = jnp.where(lane >= DQ_D, qh, 0.0)
        row_head = lax.shift_right_logical(lax.broadcasted_iota(jnp.int32, (rows, P * H_D), 0), grp.bit_length() - 1)
        col_head = lax.broadcasted_iota(jnp.int32, (rows, P * H_D), 1) & (H_D - 1)
        maskb[...] = jnp.where(row_head == col_head, 0.0, NEG_INF)
        m_s[...] = jnp.full(m_s.shape, NEG_INF, F32)
        l_s[...] = jnp.zeros(l_s.shape, F32)
        acc[...] = jnp.zeros(acc.shape, F32)

    qb = qm[...].astype(BF16)
    mb = maskb[...]
    s_parts = [_dot_nt(qb, k_refs[r][...].reshape(P * H_D, DV_D).astype(BF16)) + mb for r in range(R)]
    pr, alpha = _softmax_step(jnp.concatenate(s_parts, axis=1), m_s, l_s)
    pv = None
    for r in range(R):
        vv = v_refs[r][...].reshape(P * H_D, DV_D).astype(BF16)
        t = jnp.dot(pr[:, r * P * H_D:(r + 1) * P * H_D], vv, preferred_element_type=F32)
        pv = t if pv is None else pv + t
    acc[...] = alpha * acc[...] + pv

    @pl.when(p == pl.num_programs(1) - 1)
    def _():
        kpad[...] = jnp.zeros(kpad.shape, F32)
        kpad[0:Ts * H_D, :] = kn_ref[...]
        vpad[...] = jnp.zeros(vpad.shape, F32)
        vpad[0:Ts * H_D, :] = vn_ref[...]
        n_col = kpad.shape[0]
        r_i = lax.broadcasted_iota(jnp.int32, (rows, n_col), 0)
        j_i = lax.broadcasted_iota(jnp.int32, (rows, n_col), 1)
        key = lax.shift_right_logical(j_i, H_D.bit_length() - 1)
        row_head = lax.shift_right_logical(r_i, grp.bit_length() - 1)
        ok = ((j_i & (H_D - 1)) == row_head) & (key <= (r_i & (SUBLANES - 1))) & (key < Ts)
        s = jnp.where(ok, _dot_nt(qb, kpad[...].astype(BF16)), NEG_INF)
        pr, alpha = _softmax_step(s, m_s, l_s)
        o_full = (alpha * acc[...] + jnp.dot(pr, vpad[...].astype(BF16), preferred_element_type=F32)) / l_s[...]
        z = z_ref[...]
        lam = _lambda(lq1, lk1, lq2, lk2, lam_init)
        for h in range(H_D):
            cs = slice(h * DV_D, (h + 1) * DV_D)
            o = o_full[h * grp:h * grp + SUBLANES, :] - lam * o_full[h * grp + SUBLANES:(h + 1) * grp, :]
            o = o * lax.rsqrt(jnp.mean(o * o, axis=1, keepdims=True) + EPS) * dg_ref[...]
            o_ref[:, cs] = o[0:Ts, :] * (1.0 - lam_init) * _silu(z[:, cs])


def _decode_diff(page_table, k_cache, v_cache, layer, q, k_new, v_new, z, lam_params, lam_init, R):
    Bs, n_pages = page_table.shape
    _, Ts, C = q.shape
    P = k_cache.shape[2]
    rows = 2 * H_D * SUBLANES
    tok = pl.BlockSpec((None, Ts, C), lambda b, p, pt: (b, 0, 0))
    new = pl.BlockSpec((None, Ts * H_D, DV_D), lambda b, p, pt: (b, 0, 0))
    small = [pl.BlockSpec((1, DQ_D), lambda b, p, pt: (0, 0))] * 4 + [pl.BlockSpec((1, DV_D), lambda b, p, pt: (0, 0))]
    pages = [pl.BlockSpec((None, None, P, H_D, DV_D), lambda b, p, pt, r=r: (layer, pt[b, p * R + r], 0, 0, 0))
             for r in range(R)]
    grid_spec = pltpu.PrefetchScalarGridSpec(
        num_scalar_prefetch=1, grid=(Bs, n_pages // R),
        in_specs=[tok, new, new, tok] + small + pages + pages, out_specs=tok,
        scratch_shapes=[pltpu.VMEM((rows, DV_D), F32), pltpu.VMEM((rows, 1), F32), pltpu.VMEM((rows, 1), F32),
                        pltpu.VMEM((rows, DV_D), F32), pltpu.VMEM((rows, P * H_D), F32),
                        pltpu.VMEM((2 * SUBLANES * H_D, DV_D), F32), pltpu.VMEM((2 * SUBLANES * H_D, DV_D), F32)])
    return pl.pallas_call(
        functools.partial(_decode_diff_body, R, Ts, lam_init), grid_spec=grid_spec,
        out_shape=jax.ShapeDtypeStruct((Bs, Ts, C), F32),
        compiler_params=_cparams(("arbitrary", "arbitrary")), name="decode_diff",
    )(page_table, q, k_new, v_new, z, *[a.reshape(1, -1) for a in lam_params], *([k_cache] * R), *([v_cache] * R))


_EVEN_SRC = ((0, 5 * W_M), (5 * W_M + 2 * H_M, 5 * W_M + 2 * H_M + 4 * W_F),
             (5 * W_M, 5 * W_M + 2 * H_M), (5 * W_M + 2 * H_M + 4 * W_F, 5 * W_M + 2 * H_M + 4 * W_F + H_F))
_N_GATES = 2 * H_M + H_F


def _even_weights(w_in, b_in):
    pad = LANES - _N_GATES
    w = jnp.concatenate([w_in[:, a:b] for a, b in _EVEN_SRC] + [jnp.zeros((w_in.shape[0], pad), w_in.dtype)], axis=1)
    b = jnp.concatenate([b_in[a:b] for a, b in _EVEN_SRC] + [jnp.zeros((pad,), b_in.dtype)])
    return w.astype(BF16), b


def _even_plan(attn_dtype, q_scale, kv_kind=F32):
    kv = (kv_kind,) if attn_dtype == F32 else (kv_kind, attn_dtype)
    return ((0, 3 * W_M, (F32,), 1.0),
            (3 * W_M, 2 * W_M, (F32,), 1.0),
            (5 * W_M, W_F, (attn_dtype,), q_scale),
            (5 * W_M + W_F, W_F, kv, 1.0),
            (5 * W_M + 2 * W_F, W_F, kv, 1.0),
            (5 * W_M + 3 * W_F, W_F, (F32,), 1.0),
            (5 * W_M + 4 * W_F, LANES, (F32,), 1.0))


def _odd_plan(attn_dtype, q_scale, kv_kind=F32):
    kv = (kv_kind,) if attn_dtype == F32 else (kv_kind, attn_dtype)
    return ((0, W_D, (attn_dtype,), q_scale), (W_D, W_D, kv, 1.0), (2 * W_D, W_D, kv, 1.0),
            (3 * W_D, W_D, (F32,), 1.0))


def _tile(n, pref):
    t = min(n, pref)
    while n % t:
        t //= 2
    return t


def kernel(x_prompt, x_sample, cache_fox_k, cache_fox_v, cache_fox_logf, cache_diff_k, cache_diff_v, state_mlstm_c, state_mlstm_n, state_mlstm_m, state_mlstm_conv, page_table, norm_g, final_norm_g, w_in_even, b_in_even, conv_w, conv_b, mlstm_norm_g, w_out_even, w_in_odd, lambda_q1, lambda_k1, lambda_q2, lambda_k2, diff_norm_g, w_out_odd):
    B, T, D = x_prompt.shape
    Bs, Ts, _ = x_sample.shape
    depth = norm_g.shape[0]
    n_even, n_odd = (depth + 1) // 2, depth // 2
    n_pool, P = cache_fox_k.shape[1], cache_fox_k.shape[2]
    n_pages = page_table.shape[1]
    Mp, Ms = B * T, Bs * Ts
    tm_p, tm_s = _tile(Mp, 256), _tile(Ms, 256)
    tq = _tile(T, 512)
    q_scale = LOG2E * DH_F ** -0.5
    r_fox, r_diff = _tile(n_pages, 16), _tile(n_pages, 8)
    fox_kT_cache = jnp.transpose(cache_fox_k, (0, 1, 3, 4, 2))
    fox_vT_cache = jnp.transpose(cache_fox_v, (0, 1, 3, 4, 2))
    fox_lfT_cache = jnp.transpose(cache_fox_logf, (0, 1, 3, 2))

    hp = x_prompt.reshape(Mp, D)
    hs = x_sample.reshape(Ms, D)
    names = ("fox_k", "fox_v", "fox_logf", "diff_k", "diff_v", "mlstm_c", "mlstm_n", "mlstm_m", "mlstm_conv")
    outs_p = {nm: [] for nm in names}
    outs_s = {nm: [] for nm in names}

    for l in range(depth):
        i = l // 2
        last = l == depth - 1
        fg = final_norm_g if last else None
        if l % 2 == 0:
            w, b = _even_weights(w_in_even[i], b_in_even[i])
            wo = w_out_even[i].astype(BF16)
            wo_m, wo_f = wo[:W_M], wo[W_M:]
            qkv, oz, qf, kT_p, kfb, vT_p, vfb, zf, gt = _proj(
                hp, norm_g[l], w, b, _even_plan(BF16, q_scale, TIME_MINOR), tm_p,
                (i, n_even, T, None if i == 0 else (kT_p, vT_p)))
            hm, c_p, n_p, m_p, cv_p = _mlstm(qkv.reshape(B, T, -1), oz.reshape(B, T, -1), gt.reshape(B, T, -1),
                                             conv_w[i], conv_b[i], mlstm_norm_g[i], None, BF16)
            lff, ft = _fcum(gt.reshape(B, T, LANES))
            hf = _attn("fox", qf.reshape(B, T, W_F), kfb.reshape(B, T, W_F), vfb.reshape(B, T, W_F),
                       zf.reshape(B, T, W_F), (_fox_bias_layout(ft, tq),), 0.0, tq)
            hp = _outproj(hp, [hm.reshape(Mp, W_M), hf.reshape(Mp, W_F)], [wo_m, wo_f], fg, tm_p)
            outs_p["fox_logf"].append(lff[:, :, 2 * H_M:2 * H_M + H_F])
            outs_p["mlstm_c"].append(c_p)
            outs_p["mlstm_n"].append(n_p)
            outs_p["mlstm_m"].append(m_p[:, :, 0])
            outs_p["mlstm_conv"].append(cv_p)
            qkv, oz, qf, kf, vf, zf, gt = _proj(hs, norm_g[l], w, b, _even_plan(F32, 1.0), tm_s)
            state = (state_mlstm_c[i], state_mlstm_n[i],
                     jnp.broadcast_to(state_mlstm_m[i][:, :, None], (Bs, H_M, LANES)), state_mlstm_conv[i])
            hm, c_s, n_s, m_s, cv_s = _mlstm(qkv.reshape(Bs, Ts, -1), oz.reshape(Bs, Ts, -1), gt.reshape(Bs, Ts, -1),
                                             conv_w[i], conv_b[i], mlstm_norm_g[i], state, F32)
            hf, lffn = _decode_fox(page_table, fox_kT_cache, fox_vT_cache, fox_lfT_cache, i, qf.reshape(Bs, Ts, W_F),
                                   kf.reshape(Bs, Ts, W_F), vf.reshape(Bs, Ts, W_F), zf.reshape(Bs, Ts, W_F),
                                   gt.reshape(Bs, Ts, LANES), r_fox)
            hs = _outproj(hs, [hm.reshape(Ms, W_M), hf.reshape(Ms, W_F)], [wo_m, wo_f], fg, tm_s)
            outs_s["fox_k"].append(kf.reshape(Bs, Ts, H_F, DH_F))
            outs_s["fox_v"].append(vf.reshape(Bs, Ts, H_F, DH_F))
            outs_s["fox_logf"].append(lffn[:, :, 2 * H_M:2 * H_M + H_F])
            outs_s["mlstm_c"].append(c_s)
            outs_s["mlstm_n"].append(n_s)
            outs_s["mlstm_m"].append(m_s[:, :, 0])
            outs_s["mlstm_conv"].append(cv_s)
        else:
            lam_init = 0.8 - 0.6 * float(np.exp(-0.3 * l))
            w = w_in_odd[i].astype(BF16)
            wo = w_out_odd[i].astype(BF16)
            lam_p = (lambda_q1[i], lambda_k1[i], lambda_q2[i], lambda_k2[i], diff_norm_g[i])
            q, k_p, kb, v_p, vb, z = _proj(hp, norm_g[l], w, None, _odd_plan(BF16, q_scale, HEAD_MAJOR), tm_p,
                                           (i, n_odd, T, None if i == 0 else (k_p, v_p)))
            o = _attn("diff", q.reshape(B, T, W_D), kb.reshape(B, T, W_D), vb.reshape(B, T, W_D),
                      z.reshape(B, T, W_D), lam_p, lam_init, tq)
            hp = _outproj(hp, [o.reshape(Mp, W_D)], [wo], fg, tm_p)
            q, k, v, z = _proj(hs, norm_g[l], w, None, _odd_plan(F32, 1.0), tm_s)
            o = _decode_diff(page_table, cache_diff_k, cache_diff_v, i, q.reshape(Bs, Ts, W_D),
                             k.reshape(Bs, Ts * H_D, DV_D), v.reshape(Bs, Ts * H_D, DV_D), z.reshape(Bs, Ts, W_D),
                             lam_p, lam_init, r_diff)
            hs = _outproj(hs, [o.reshape(Ms, W_D)], [wo], fg, tm_s)
            outs_s["diff_k"].append(k.reshape(Bs, Ts, H_D, 2 * DQ_D))
            outs_s["diff_v"].append(v.reshape(Bs, Ts, H_D, DV_D))

    stacked_p = {
        "fox_k": kT_p.reshape(n_even, B, H_F, DH_F, T).transpose(0, 1, 4, 2, 3),
        "fox_v": vT_p.reshape(n_even, B, H_F, DH_F, T).transpose(0, 1, 4, 2, 3),
        "diff_k": k_p.reshape(n_odd, B, T, H_D, 2 * DQ_D),
        "diff_v": v_p.reshape(n_odd, B, T, H_D, DV_D),
    }
    return (hp.reshape(B, T, D), hs.reshape(Bs, Ts, D),
            *[stacked_p[nm] if nm in stacked_p else jnp.stack(outs_p[nm]) for nm in names],
            *[jnp.stack(outs_s[nm]) for nm in names])
```

```python
import functools

import numpy as np
import jax
import jax.numpy as jnp
from jax import lax
from jax.experimental import pallas as pl
from jax.experimental.pallas import tpu as pltpu

F32 = jnp.float32
BF16 = jnp.bfloat16
NEG_INF = float("-inf")

EPS = 1e-6
LOG2E = 1.4426950408889634
H_M, DH_M = 4, 128
H_F, DH_F = 8, 64
H_D, DV_D, DQ_D = 8, 128, 64
CONV_W = 4
M_CHUNK = 128
LANES = 128
SUBLANES = 8
VMEM_LIMIT = 52 * 1024 * 1024

W_M = H_M * DH_M
W_F = H_F * DH_F
W_D = H_D * DV_D


def _cparams(sem):
    return pltpu.CompilerParams(dimension_semantics=sem, vmem_limit_bytes=VMEM_LIMIT)


def _sigmoid(x):
    return 1.0 / (1.0 + jnp.exp(-x))


def _silu(x):
    return x * _sigmoid(x)


def _log_sigmoid(x):
    return jnp.minimum(x, 0.0) - jnp.log(1.0 + jnp.exp(-jnp.abs(x)))


def _dot_nt(a, b):
    return lax.dot_general(a, b, (((1,), (1,)), ((), ())), preferred_element_type=F32)


TIME_MINOR = "time_minor"
HEAD_MAJOR = "head_major"


def _proj_body(plan, has_bias, n_prev, x_ref, g_ref, w_ref, *rest):
    if has_bias:
        b_ref, *rest = rest
    out_refs = rest[n_prev:]
    x = x_ref[...]
    xn = x * lax.rsqrt(jnp.mean(x * x, axis=-1, keepdims=True) + EPS) * g_ref[...]
    xn = xn.astype(BF16)
    k = 0
    for c0, width, kinds, scale in plan:
        y = jnp.dot(xn, w_ref[:, c0:c0 + width], preferred_element_type=F32)
        if has_bias:
            y = y + b_ref[:, c0:c0 + width]
        if scale != 1.0:
            y = y * scale
        for kind in kinds:
            if kind == TIME_MINOR:
                out_refs[k][...] = y.T
            elif kind == HEAD_MAJOR:
                out_refs[k][...] = y.reshape(out_refs[k].shape)
            else:
                out_refs[k][...] = y.astype(kind)
            k += 1


def _proj(x, g, w, b, plan, tm, stack=None):
    M, D = x.shape
    N = w.shape[1]
    has_bias = b is not None
    layer, n_layers, T, prev = stack if stack is not None else (0, 1, M, None)
    out_shape, out_specs, stacked = [], [], []
    for _, width, kinds, _ in plan:
        for kind in kinds:
            if kind == TIME_MINOR:
                nt = T // tm
                out_shape.append(jax.ShapeDtypeStruct((n_layers, M // T, width, T), F32))
                out_specs.append(pl.BlockSpec((None, None, width, tm), lambda i: (layer, i // nt, 0, i % nt)))
                stacked.append(len(out_shape) - 1)
            elif kind == HEAD_MAJOR:
                out_shape.append(jax.ShapeDtypeStruct((n_layers, M, H_D, width // H_D), F32))
                out_specs.append(pl.BlockSpec((None, tm, H_D, width // H_D), lambda i: (layer, i, 0, 0)))
                stacked.append(len(out_shape) - 1)
            else:
                out_shape.append(jax.ShapeDtypeStruct((M, width), kind))
                out_specs.append(pl.BlockSpec((tm, width), lambda i: (i, 0)))
    in_specs = [pl.BlockSpec((tm, D), lambda i: (i, 0)),
                pl.BlockSpec((1, D), lambda i: (0, 0)),
                pl.BlockSpec((D, N), lambda i: (0, 0))]
    args = [x, g.reshape(1, D), w]
    if has_bias:
        in_specs.append(pl.BlockSpec((1, N), lambda i: (0, 0)))
        args.append(b.reshape(1, N))
    aliases = {}
    if prev is not None:
        for a, o in zip(prev, stacked):
            aliases[len(args)] = o
            in_specs.append(pl.BlockSpec(memory_space=pl.ANY))
            args.append(a)
    return pl.pallas_call(
        functools.partial(_proj_body, plan, has_bias, len(aliases)),
        grid=(M // tm,), in_specs=in_specs, out_specs=out_specs, out_shape=out_shape,
        input_output_aliases=aliases, compiler_params=_cparams(("arbitrary",)), name="proj",
    )(*args)


def _mlstm_body(Lv, has_state, qkv_ref, oz_ref, gt_ref, cw_ref, cb_ref, mg_ref, *rest):
    L = M_CHUNK
    if has_state:
        c0_ref, n0_ref, m0_ref, cv0_ref, *rest = rest
    h_ref, cN_ref, nN_ref, mN_ref, cvN_ref, ubuf, *pads = rest
    c = pl.program_id(1)

    @pl.when(c == 0)
    def _():
        ubuf[...] = jnp.zeros(ubuf.shape, F32)
        if has_state:
            ubuf[SUBLANES - (CONV_W - 1):SUBLANES, :] = cv0_ref[...]
            cN_ref[...] = c0_ref[...]
            nN_ref[...] = n0_ref[...]
            mN_ref[...] = m0_ref[...]
        else:
            cN_ref[...] = jnp.zeros(cN_ref.shape, F32)
            nN_ref[...] = jnp.zeros(nN_ref.shape, F32)
            mN_ref[...] = jnp.zeros(mN_ref.shape, F32)

    ubuf[SUBLANES:SUBLANES + Lv, :] = qkv_ref[:, 0:2 * W_M]
    acc = None
    for j in range(CONV_W):
        r0 = SUBLANES - (CONV_W - 1) + j
        term = ubuf[r0:r0 + L, :] * cw_ref[j:j + 1, :]
        acc = term if acc is None else acc + term
    y = acc + cb_ref[...]
    tail = ubuf[SUBLANES + Lv - (CONV_W - 1):SUBLANES + Lv, :]
    ubuf[SUBLANES - (CONV_W - 1):SUBLANES, :] = tail
    cvN_ref[...] = tail
    qk = _silu(y)

    if Lv == L:
        G = gt_ref[...]
        V = qkv_ref[:, 2 * W_M:3 * W_M]
    else:
        gbuf, vbuf = pads
        gbuf[...] = jnp.zeros(gbuf.shape, F32)
        gbuf[0:Lv, :] = gt_ref[...]
        vbuf[...] = jnp.zeros(vbuf.shape, F32)
        vbuf[0:Lv, :] = qkv_ref[:, 2 * W_M:3 * W_M]
        G = gbuf[...]
        V = vbuf[...]
    LF = _log_sigmoid(G)
    GT = G.T
    LFT = LF.T
    r_i = lax.broadcasted_iota(jnp.int32, (L, L), 0)
    c_i = lax.broadcasted_iota(jnp.int32, (L, L), 1)
    tril = r_i >= c_i
    triu = r_i <= c_i
    rowv = lax.broadcasted_iota(jnp.int32, (L, 1), 0) < Lv
    colv = lax.broadcasted_iota(jnp.int32, (1, L), 1) < Lv

    for h in range(H_M):
        hs = slice(h * DH_M, (h + 1) * DH_M)
        q = qk[:, h * DH_M:(h + 1) * DH_M]
        k = qk[:, W_M + h * DH_M:W_M + (h + 1) * DH_M] * (DH_M ** -0.5)
        vb = V[:, hs].astype(BF16)
        ig_row = GT[h:h + 1, :]
        lf_row = LFT[H_M + h:H_M + h + 1, :]
        ig_col = G[:, h:h + 1]
        lf_col = LF[:, H_M + h:H_M + h + 1]
        if Lv < L:
            lf_row = jnp.where(colv, lf_row, 0.0)
            lf_col = jnp.where(rowv, lf_col, 0.0)
            ig_row = jnp.where(colv, ig_row, NEG_INF)
            ig_col = jnp.where(rowv, ig_col, NEG_INF)
        F_col = jnp.sum(jnp.where(tril, lf_row, 0.0), axis=1, keepdims=True)
        F_row = jnp.sum(jnp.where(triu, lf_col, 0.0), axis=0, keepdims=True)
        Dm = jnp.where(tril, F_col - F_row + ig_row, NEG_INF)
        m_prev = mN_ref[h:h + 1, 0:1]
        inter = F_col + m_prev
        m_t = jnp.maximum(jnp.max(Dm, axis=1, keepdims=True), inter)
        qb = q.astype(BF16)
        S = jnp.exp(Dm - m_t) * _dot_nt(qb, k.astype(BF16))
        w_inter = jnp.exp(inter - m_t)
        C = cN_ref[h]
        n_row = nN_ref[h:h + 1, :]
        num = (jnp.dot(S.astype(BF16), vb, preferred_element_type=F32)
               + w_inter * jnp.dot(qb, C.astype(BF16), preferred_element_type=F32))
        den = jnp.sum(S, axis=1, keepdims=True) + w_inter * jnp.sum(q * n_row, axis=1, keepdims=True)
        hh = num / jnp.maximum(jnp.abs(den), jnp.exp(-m_t))

        m_new = m_t[Lv - 1:Lv, :]
        F_last = F_col[Lv - 1:Lv, :]
        w_s = jnp.exp(F_last - F_col + ig_col - m_new)
        decay = jnp.exp(F_last + m_prev - m_new)
        kw = k * w_s
        cN_ref[h] = decay * C + jnp.dot(kw.T.astype(BF16), vb, preferred_element_type=F32)
        nN_ref[h:h + 1, :] = decay * n_row + jnp.sum(kw, axis=0, keepdims=True)
        mN_ref[h:h + 1, :] = jnp.broadcast_to(m_new, (1, LANES))

        hn = hh * lax.rsqrt(jnp.mean(hh * hh, axis=1, keepdims=True) + EPS) * mg_ref[:, hs]
        hn = hn[0:Lv, :]
        o = oz_ref[:, h * DH_M:(h + 1) * DH_M]
        z = oz_ref[:, W_M + h * DH_M:W_M + (h + 1) * DH_M]
        h_ref[:, hs] = (hn * _sigmoid(o) * _silu(z)).astype(h_ref.dtype)


def _mlstm(qkv, oz, gt, conv_w, conv_b, mnorm_g, state, out_dtype):
    B, T, _ = qkv.shape
    Lv = min(T, M_CHUNK)
    nc = T // Lv
    has_state = state is not None
    row = lambda b, c: (b, c, 0)
    fix2 = lambda b, c: (0, 0)
    per_b3 = lambda b, c: (b, 0, 0)
    per_b4 = lambda b, c: (b, 0, 0, 0)
    in_specs = [pl.BlockSpec((None, Lv, 3 * W_M), row),
                pl.BlockSpec((None, Lv, 2 * W_M), row),
                pl.BlockSpec((None, Lv, LANES), row),
                pl.BlockSpec((CONV_W, 2 * W_M), fix2),
                pl.BlockSpec((1, 2 * W_M), fix2),
                pl.BlockSpec((1, W_M), fix2)]
    args = [qkv, oz, gt, conv_w, conv_b.reshape(1, -1), mnorm_g.reshape(1, -1)]
    state_specs = [pl.BlockSpec((None, H_M, DH_M, DH_M), per_b4),
                   pl.BlockSpec((None, H_M, DH_M), per_b3),
                   pl.BlockSpec((None, H_M, LANES), per_b3),
                   pl.BlockSpec((None, CONV_W - 1, 2 * W_M), per_b3)]
    if has_state:
        in_specs += state_specs
        args += list(state)
    out_shape = [jax.ShapeDtypeStruct((B, T, W_M), out_dtype),
                 jax.ShapeDtypeStruct((B, H_M, DH_M, DH_M), F32),
                 jax.ShapeDtypeStruct((B, H_M, DH_M), F32),
                 jax.ShapeDtypeStruct((B, H_M, LANES), F32),
                 jax.ShapeDtypeStruct((B, CONV_W - 1, 2 * W_M), F32)]
    out_specs = [pl.BlockSpec((None, Lv, W_M), row)] + state_specs
    scratch = [pltpu.VMEM((SUBLANES + M_CHUNK, 2 * W_M), F32)]
    if Lv < M_CHUNK:
        scratch += [pltpu.VMEM((M_CHUNK, LANES), F32), pltpu.VMEM((M_CHUNK, W_M), F32)]
    return pl.pallas_call(
        functools.partial(_mlstm_body, Lv, has_state),
        grid=(B, nc), in_specs=in_specs, out_specs=out_specs, out_shape=out_shape,
        scratch_shapes=scratch, compiler_params=_cparams(("arbitrary", "arbitrary")), name="mlstm",
    )(*args)


def _fcum_body(n_chunks, g_ref, lff_ref, ft_ref):
    L = LANES
    triu = lax.broadcasted_iota(jnp.int32, (L, L), 0) <= lax.broadcasted_iota(jnp.int32, (L, L), 1)

    def chunk(c, carry):
        r0 = pl.multiple_of(c * L, L)
        LF = _log_sigmoid(g_ref[pl.ds(r0, L), :])
        lff_ref[pl.ds(r0, L), :] = LF
        new = []
        for h in range(H_F):
            col = LF[:, H_F + h:H_F + h + 1]
            fr = jnp.sum(jnp.where(triu, col, 0.0), axis=0, keepdims=True) + carry[h]
            ft_ref[c, h:h + 1, :] = fr * LOG2E
            new.append(fr[:, L - 1:L])
        return tuple(new)

    lax.fori_loop(0, n_chunks, chunk, tuple(jnp.zeros((1, 1), F32) for _ in range(H_F)))


def _fcum(gt):
    B, T, _ = gt.shape
    nc = T // LANES
    return pl.pallas_call(
        functools.partial(_fcum_body, nc),
        grid=(B,),
        in_specs=[pl.BlockSpec((None, T, LANES), lambda b: (b, 0, 0))],
        out_specs=[pl.BlockSpec((None, T, LANES), lambda b: (b, 0, 0)),
                   pl.BlockSpec((None, nc, H_F, LANES), lambda b: (b, 0, 0, 0))],
        out_shape=[jax.ShapeDtypeStruct((B, T, LANES), F32), jax.ShapeDtypeStruct((B, nc, H_F, LANES), F32)],
        compiler_params=_cparams(("arbitrary",)), name="fcum",
    )(gt)


def _fox_bias_layout(ft, tq):
    B, nc, _, L = ft.shape
    per = tq // L
    x = ft.reshape(B, nc // per, per, H_F // 2, 2, L)
    return x.transpose(0, 3, 1, 4, 2, 5).reshape(B, H_F // 2, nc // per, 2, tq)


def _lambda(lq1_ref, lk1_ref, lq2_ref, lk2_ref, lam_init):
    return (jnp.exp(jnp.sum(lq1_ref[...] * lk1_ref[...], axis=1, keepdims=True))
            - jnp.exp(jnp.sum(lq2_ref[...] * lk2_ref[...], axis=1, keepdims=True)) + lam_init)


def _attn_body(mode, tq, lam_init, q_ref, k_ref, v_ref, z_ref, *rest):
    if mode == "fox":
        ft_ref, o_ref, q2, s_scr, p_scr, m_s, acc = rest
    else:
        lq1, lk1, lq2, lk2, dg_ref, o_ref, q2, s_scr, p_scr, m_s, acc = rest
    qi = pl.program_id(2)
    rows = 2 * tq
    sb = min(tq, 64)
    lane_row = lax.broadcasted_iota(jnp.int32, (1, LANES), 1)
    q = q_ref[...]
    q2[0:tq, :] = q * jnp.where(lane_row < DH_F, 1.0, 0.0).astype(BF16)
    q2[tq:rows, :] = q * jnp.where(lane_row >= DH_F, 1.0, 0.0).astype(BF16)
    m_s[...] = jnp.full(m_s.shape, NEG_INF, F32)
    acc[...] = jnp.zeros(acc.shape, F32)
    p_scr[...] = jnp.zeros(p_scr.shape, BF16)
    ones = jnp.ones((tq, LANES), BF16)

    def scores(j):
        k0 = pl.multiple_of(j * tq, tq)
        k = k_ref[pl.ds(k0, tq), :]
        return jnp.concatenate([_dot_nt(q2[0:tq, :], k), _dot_nt(q2[tq:rows, :], k)], axis=0)

    def pv(j):
        k0 = pl.multiple_of(j * tq, tq)
        vext = jnp.concatenate([v_ref[pl.ds(k0, tq), :], ones], axis=1)
        return jnp.dot(p_scr[...], vext, preferred_element_type=F32)

    def softmax_rows(j, pend, masked, keep):
        outs = []
        for r0 in range(0, rows, sb):
            s = s_scr[r0:r0 + sb, :]
            if mode == "fox":
                a = 0 if r0 < tq else 1
                s = s - ft_ref[j, a:a + 1, :]
            if masked:
                r_i = lax.broadcasted_iota(jnp.int32, (sb, tq), 0) + (r0 & (tq - 1))
                c_i = lax.broadcasted_iota(jnp.int32, (sb, tq), 1)
                s = jnp.where(r_i >= c_i, s, NEG_INF)
            m_old = m_s[r0:r0 + sb, :]
            m_new = jnp.maximum(m_old, jnp.max(s, axis=1, keepdims=True))
            p = jnp.exp2(s - jnp.concatenate([m_new] * (tq // LANES), axis=1)).astype(BF16)
            alpha = jnp.exp2(m_old - m_new)
            m_s[r0:r0 + sb, :] = m_new
            a_new = jnp.concatenate([alpha, alpha], axis=1) * (acc[r0:r0 + sb, :] + pend[r0:r0 + sb, :])
            if keep:
                acc[r0:r0 + sb, :] = a_new
            else:
                outs.append(a_new)
            p_scr[r0:r0 + sb, :] = p
        return outs

    def body(j, c):
        pend = pv(jnp.maximum(j - 1, 0))
        s_scr[...] = scores(j)
        softmax_rows(j, pend, False, True)
        return c

    lax.fori_loop(0, qi, body, 0)
    s_scr[...] = scores(qi)
    pend = pv(jnp.maximum(qi - 1, 0))
    a = jnp.concatenate(softmax_rows(qi, pend, True, False), axis=0)
    a = a + pv(qi)
    o0 = a[0:tq, 0:LANES] / a[0:tq, LANES:2 * LANES]
    o1 = a[tq:rows, 0:LANES] / a[tq:rows, LANES:2 * LANES]
    z = z_ref[...]
    if mode == "fox":
        lane = lax.broadcasted_iota(jnp.int32, (tq, LANES), 1)
        o = jnp.where(lane < DH_F, o0, o1)
        o_ref[...] = (o * _silu(z)).astype(o_ref.dtype)
    else:
        lam = _lambda(lq1, lk1, lq2, lk2, lam_init)
        o = o0 - lam * o1
        o = o * lax.rsqrt(jnp.mean(o * o, axis=1, keepdims=True) + EPS) * dg_ref[...]
        o_ref[...] = (o * (1.0 - lam_init) * _silu(z)).astype(o_ref.dtype)


def _attn(mode, q, k, v, z, extra, lam_init, tq):
    B, T, C = q.shape
    G = C // LANES
    scratch = [pltpu.VMEM((2 * tq, LANES), BF16), pltpu.VMEM((2 * tq, tq), F32), pltpu.VMEM((2 * tq, tq), BF16),
               pltpu.VMEM((2 * tq, LANES), F32), pltpu.VMEM((2 * tq, 2 * LANES), F32)]
    qspec = pl.BlockSpec((None, tq, LANES), lambda b, g, i: (b, i, g))
    kspec = pl.BlockSpec((None, T, LANES), lambda b, g, i: (b, 0, g))
    in_specs = [qspec, kspec, kspec, qspec]
    if mode == "fox":
        (ft,) = extra
        in_specs.append(pl.BlockSpec((None, None, T // tq, 2, tq), lambda b, g, i: (b, g, 0, 0, 0)))
        args = [q, k, v, z, ft]
    else:
        lq1, lk1, lq2, lk2, dg = extra
        in_specs += [pl.BlockSpec((1, DQ_D), lambda b, g, i: (0, 0))] * 4
        in_specs.append(pl.BlockSpec((1, DV_D), lambda b, g, i: (0, 0)))
        args = [q, k, v, z] + [a.reshape(1, -1) for a in (lq1, lk1, lq2, lk2, dg)]
    return pl.pallas_call(
        functools.partial(_attn_body, mode, tq, lam_init),
        grid=(B, G, T // tq), in_specs=in_specs, out_specs=qspec,
        out_shape=jax.ShapeDtypeStruct((B, T, C), BF16), scratch_shapes=scratch,
        compiler_params=_cparams(("arbitrary", "arbitrary", "arbitrary")), name="attn_" + mode,
    )(*args)


def _outproj_body(n_in, final, x_ref, *rest):
    a_refs, w_refs = rest[:n_in], rest[n_in:2 * n_in]
    rest = rest[2 * n_in:]
    acc = None
    for a_ref, w_ref in zip(a_refs, w_refs):
        t = jnp.dot(a_ref[...].astype(BF16), w_ref[...], preferred_element_type=F32)
        acc = t if acc is None else acc + t
    y = x_ref[...] + acc
    if final:
        g_ref, o_ref = rest
        y = y * lax.rsqrt(jnp.mean(y * y, axis=-1, keepdims=True) + EPS) * g_ref[...]
    else:
        (o_ref,) = rest
    o_ref[...] = y


def _outproj(x, acts, ws, final_g, tm):
    M, D = x.shape
    n_in = len(acts)
    final = final_g is not None
    in_specs = [pl.BlockSpec((tm, D), lambda i: (i, 0))]
    in_specs += [pl.BlockSpec((tm, a.shape[1]), lambda i: (i, 0)) for a in acts]
    in_specs += [pl.BlockSpec(w.shape, lambda i: (0, 0)) for w in ws]
    args = [x, *acts, *ws]
    if final:
        in_specs.append(pl.BlockSpec((1, D), lambda i: (0, 0)))
        args.append(final_g.reshape(1, D))
    return pl.pallas_call(
        functools.partial(_outproj_body, n_in, final),
        grid=(M // tm,), in_specs=in_specs, out_specs=pl.BlockSpec((tm, D), lambda i: (i, 0)),
        out_shape=jax.ShapeDtypeStruct((M, D), F32),
        compiler_params=_cparams(("arbitrary",)), name="outproj",
    )(*args)


def _softmax_step(s, m_s, l_s):
    m_old = m_s[...]
    m_new = jnp.maximum(m_old, jnp.max(s, axis=1, keepdims=True))
    pr = jnp.exp(s - m_new)
    alpha = jnp.exp(m_old - m_new)
    l_s[...] = alpha * l_s[...] + jnp.sum(pr, axis=1, keepdims=True)
    m_s[...] = m_new
    return pr.astype(BF16), alpha


def _decode_fox_body(R, Ts, pt_ref, q_ref, kn_ref, vn_ref, z_ref, g_ref, *refs):
    lf_refs, k_refs, v_refs = refs[:R], refs[R:2 * R], refs[2 * R:3 * R]
    o_ref, lffn_ref, qm, m_s, l_s, acc, carry, kpad, vpad, gpad = refs[3 * R:]
    P = LANES
    rows = H_F * SUBLANES
    p = pl.program_id(1)

    @pl.when(p == 0)
    def _():
        lane = lax.broadcasted_iota(jnp.int32, (Ts, W_F), 1)
        q = q_ref[...] * (DH_F ** -0.5)
        qm[...] = jnp.zeros(qm.shape, F32)
        for h in range(H_F):
            qm[h * SUBLANES:h * SUBLANES + Ts, :] = jnp.where((lane >= h * DH_F) & (lane < (h + 1) * DH_F), q, 0.0)
        m_s[...] = jnp.full(m_s.shape, NEG_INF, F32)
        l_s[...] = jnp.zeros(l_s.shape, F32)
        acc[...] = jnp.zeros(acc.shape, F32)
        carry[...] = jnp.zeros(carry.shape, F32)

    qb = qm[...].astype(BF16)

    def bias_rows(f):
        return jnp.concatenate([jnp.broadcast_to(f[h:h + 1, :], (SUBLANES, P)) for h in range(H_F)], axis=0)

    lf = jnp.concatenate([lf_refs[r][...] for r in range(R)], axis=0)
    tri = (lax.broadcasted_iota(jnp.int32, (P, P), 0) <= lax.broadcasted_iota(jnp.int32, (P, P), 1))
    tri = jnp.where(tri, 1.0, 0.0).astype(BF16)
    hi = lf.astype(BF16)
    rem = lf - hi.astype(F32)
    mid = rem.astype(BF16)
    lo = (rem - mid.astype(F32)).astype(BF16)
    f = (jnp.dot(hi, tri, preferred_element_type=F32) + jnp.dot(mid, tri, preferred_element_type=F32)
         + jnp.dot(lo, tri, preferred_element_type=F32))
    totals = [jnp.broadcast_to(f[r * H_F:(r + 1) * H_F, P - 1:P], (H_F, P)) for r in range(R)]
    run = carry[...]
    s_parts = []
    for r in range(R):
        kT = k_refs[r][...].reshape(W_F, P).astype(BF16)
        s_parts.append(jnp.dot(qb, kT, preferred_element_type=F32) - bias_rows(f[r * H_F:(r + 1) * H_F, :] + run))
        run = run + totals[r]
    carry[...] = run
    pr, alpha = _softmax_step(jnp.concatenate(s_parts, axis=1), m_s, l_s)
    pv = None
    for r in range(R):
        vT = v_refs[r][...].reshape(W_F, P).astype(BF16)
        t = _dot_nt(pr[:, r * P:(r + 1) * P], vT)
        pv = t if pv is None else pv + t
    acc[...] = alpha * acc[...] + pv

    @pl.when(p == pl.num_programs(1) - 1)
    def _():
        lf = _log_sigmoid(g_ref[...])
        lffn_ref[...] = lf
        gpad[...] = jnp.zeros(gpad.shape, F32)
        gpad[0:Ts, :] = lf
        lfp = gpad[...]
        triu = lax.broadcasted_iota(jnp.int32, (P, P), 0) <= lax.broadcasted_iota(jnp.int32, (P, P), 1)
        fn = [jnp.sum(jnp.where(triu, lfp[:, 2 * H_M + h:2 * H_M + h + 1], 0.0), axis=0, keepdims=True)
              + carry[h:h + 1, :] for h in range(H_F)]
        bias = jnp.concatenate([jnp.broadcast_to(f, (SUBLANES, P)) for f in fn], axis=0)
        kpad[...] = jnp.zeros(kpad.shape, F32)
        kpad[0:Ts, :] = kn_ref[...]
        vpad[...] = jnp.zeros(vpad.shape, F32)
        vpad[0:Ts, :] = vn_ref[...]
        t_i = lax.broadcasted_iota(jnp.int32, (rows, P), 0) & (SUBLANES - 1)
        j_i = lax.broadcasted_iota(jnp.int32, (rows, P), 1)
        s = _dot_nt(qb, kpad[...].astype(BF16)) - bias
        s = jnp.where((j_i <= t_i) & (j_i < Ts), s, NEG_INF)
        pr, alpha = _softmax_step(s, m_s, l_s)
        o_full = (alpha * acc[...] + jnp.dot(pr, vpad[...].astype(BF16), preferred_element_type=F32)) / l_s[...]
        z = z_ref[...]
        lane = lax.broadcasted_iota(jnp.int32, (SUBLANES, LANES), 1)
        for j in range(W_F // LANES):
            cs = slice(j * LANES, (j + 1) * LANES)
            o = jnp.where(lane < DH_F, o_full[(2 * j) * SUBLANES:(2 * j + 1) * SUBLANES, cs],
                          o_full[(2 * j + 1) * SUBLANES:(2 * j + 2) * SUBLANES, cs])
            o_ref[:, cs] = o[0:Ts, :] * _silu(z[:, cs])


def _decode_fox(page_table, kT_cache, vT_cache, lfT_cache, layer, q, k_new, v_new, z, gt, R):
    Bs, n_pages = page_table.shape
    _, Ts, C = q.shape
    P = kT_cache.shape[-1]
    rows = H_F * SUBLANES
    tok = pl.BlockSpec((None, Ts, C), lambda b, p, pt: (b, 0, 0))
    gspec = pl.BlockSpec((None, Ts, LANES), lambda b, p, pt: (b, 0, 0))
    lf_pages = [pl.BlockSpec((None, None, H_F, P), lambda b, p, pt, r=r: (layer, pt[b, p * R + r], 0, 0))
                for r in range(R)]
    kv_pages = [pl.BlockSpec((None, None, H_F, DH_F, P), lambda b, p, pt, r=r: (layer, pt[b, p * R + r], 0, 0, 0))
                for r in range(R)]
    grid_spec = pltpu.PrefetchScalarGridSpec(
        num_scalar_prefetch=1, grid=(Bs, n_pages // R),
        in_specs=[tok, tok, tok, tok, gspec] + lf_pages + kv_pages + kv_pages, out_specs=[tok, gspec],
        scratch_shapes=[pltpu.VMEM((rows, C), F32), pltpu.VMEM((rows, 1), F32), pltpu.VMEM((rows, 1), F32),
                        pltpu.VMEM((rows, C), F32), pltpu.VMEM((H_F, P), F32),
                        pltpu.VMEM((P, C), F32), pltpu.VMEM((P, C), F32), pltpu.VMEM((P, LANES), F32)])
    return pl.pallas_call(
        functools.partial(_decode_fox_body, R, Ts), grid_spec=grid_spec,
        out_shape=[jax.ShapeDtypeStruct((Bs, Ts, C), F32), jax.ShapeDtypeStruct((Bs, Ts, LANES), F32)],
        compiler_params=_cparams(("arbitrary", "arbitrary")), name="decode_fox",
    )(page_table, q, k_new, v_new, z, gt, *([lfT_cache] * R), *([kT_cache] * R), *([vT_cache] * R))


def _query_rows(Ts):
    return SUBLANES // 2 if Ts <= SUBLANES // 2 else SUBLANES


def _decode_diff_body(R, Ts, lam_init, pt_ref, q_ref, kn_ref, vn_ref, z_ref, lq1, lk1, lq2, lk2, dg_ref, *refs):
    k_refs, v_refs = refs[:R], refs[R:2 * R]
    o_ref, qm, m_s, l_s, acc, maskb, kpad, vpad = refs[2 * R:]
    P = k_refs[0].shape[0]
    tp = _query_rows(Ts)
    rows = 2 * H_D * tp
    grp = 2 * tp
    p = pl.program_id(1)

    @pl.when(p == 0)
    def _():
        lane = lax.broadcasted_iota(jnp.int32, (Ts, DV_D), 1)
        q = q_ref[...] * (DQ_D ** -0.5)
        qm[...] = jnp.zeros(qm.shape, F32)
        for h in range(H_D):
            qh = q[:, h * DV_D:(h + 1) * DV_D]
            qm[h * grp:h * grp + Ts, :] = jnp.where(lane < DQ_D, qh, 0.0)
            qm[h * grp + tp:h * grp + tp + Ts, :] = jnp.where(lane >= DQ_D, qh, 0.0)
        row_head = lax.shift_right_logical(lax.broadcasted_iota(jnp.int32, (rows, P * H_D), 0), grp.bit_length() - 1)
        col_head = lax.broadcasted_iota(jnp.int32, (rows, P * H_D), 1) & (H_D - 1)
        maskb[...] = jnp.where(row_head == col_head, 0.0, NEG_INF)
        m_s[...] = jnp.full(m_s.shape, NEG_INF, F32)
        l_s[...] = jnp.zeros(l_s.shape, F32)
        acc[...] = jnp.zeros(acc.shape, F32)

    qb = qm[...].astype(BF16)
    mb = maskb[...]
    s_parts = [_dot_nt(qb, k_refs[r][...].reshape(P * H_D, DV_D).astype(BF16)) + mb for r in range(R)]
    pr, alpha = _softmax_step(jnp.concatenate(s_parts, axis=1), m_s, l_s)
    pv = None
    for r in range(R):
        vv = v_refs[r][...].reshape(P * H_D, DV_D).astype(BF16)
        t = jnp.dot(pr[:, r * P * H_D:(r + 1) * P * H_D], vv, preferred_element_type=F32)
        pv = t if pv is None else pv + t
    acc[...] = alpha * acc[...] + pv

    @pl.when(p == pl.num_programs(1) - 1)
    def _():
        kpad[...] = jnp.zeros(kpad.shape, F32)
        kpad[0:Ts * H_D, :] = kn_ref[...]
        vpad[...] = jnp.zeros(vpad.shape, F32)
        vpad[0:Ts * H_D, :] = vn_ref[...]
        n_col = kpad.shape[0]
        r_i = lax.broadcasted_iota(jnp.int32, (rows, n_col), 0)
        j_i = lax.broadcasted_iota(jnp.int32, (rows, n_col), 1)
        key = lax.shift_right_logical(j_i, H_D.bit_length() - 1)
        row_head = lax.shift_right_logical(r_i, grp.bit_length() - 1)
        ok = ((j_i & (H_D - 1)) == row_head) & (key <= (r_i & (tp - 1))) & (key < Ts)
        s = jnp.where(ok, _dot_nt(qb, kpad[...].astype(BF16)), NEG_INF)
        pr, alpha = _softmax_step(s, m_s, l_s)
        o_full = (alpha * acc[...] + jnp.dot(pr, vpad[...].astype(BF16), preferred_element_type=F32)) / l_s[...]
        z = z_ref[...]
        lam = _lambda(lq1, lk1, lq2, lk2, lam_init)
        for h in range(H_D):
            cs = slice(h * DV_D, (h + 1) * DV_D)
            o = o_full[h * grp:h * grp + tp, :] - lam * o_full[h * grp + tp:(h + 1) * grp, :]
            o = o * lax.rsqrt(jnp.mean(o * o, axis=1, keepdims=True) + EPS) * dg_ref[...]
            o_ref[:, cs] = o[0:Ts, :] * (1.0 - lam_init) * _silu(z[:, cs])


def _decode_diff(page_table, k_cache, v_cache, layer, q, k_new, v_new, z, lam_params, lam_init, R):
    Bs, n_pages = page_table.shape
    _, Ts, C = q.shape
    P = k_cache.shape[2]
    rows = 2 * H_D * _query_rows(Ts)
    tok = pl.BlockSpec((None, Ts, C), lambda b, p, pt: (b, 0, 0))
    new = pl.BlockSpec((None, Ts * H_D, DV_D), lambda b, p, pt: (b, 0, 0))
    small = [pl.BlockSpec((1, DQ_D), lambda b, p, pt: (0, 0))] * 4 + [pl.BlockSpec((1, DV_D), lambda b, p, pt: (0, 0))]
    pages = [pl.BlockSpec((None, None, P, H_D, DV_D), lambda b, p, pt, r=r: (layer, pt[b, p * R + r], 0, 0, 0))
             for r in range(R)]
    grid_spec = pltpu.PrefetchScalarGridSpec(
        num_scalar_prefetch=1, grid=(Bs, n_pages // R),
        in_specs=[tok, new, new, tok] + small + pages + pages, out_specs=tok,
        scratch_shapes=[pltpu.VMEM((rows, DV_D), F32), pltpu.VMEM((rows, 1), F32), pltpu.VMEM((rows, 1), F32),
                        pltpu.VMEM((rows, DV_D), F32), pltpu.VMEM((rows, P * H_D), F32),
                        pltpu.VMEM((2 * SUBLANES * H_D, DV_D), F32), pltpu.VMEM((2 * SUBLANES * H_D, DV_D), F32)])
    return pl.pallas_call(
        functools.partial(_decode_diff_body, R, Ts, lam_init), grid_spec=grid_spec,
        out_shape=jax.ShapeDtypeStruct((Bs, Ts, C), F32),
        compiler_params=_cparams(("arbitrary", "arbitrary")), name="decode_diff",
    )(page_table, q, k_new, v_new, z, *[a.reshape(1, -1) for a in lam_params], *([k_cache] * R), *([v_cache] * R))


_EVEN_SRC = ((0, 5 * W_M), (5 * W_M + 2 * H_M, 5 * W_M + 2 * H_M + 4 * W_F),
             (5 * W_M, 5 * W_M + 2 * H_M), (5 * W_M + 2 * H_M + 4 * W_F, 5 * W_M + 2 * H_M + 4 * W_F + H_F))
_N_GATES = 2 * H_M + H_F


def _even_weights(w_in, b_in):
    pad = LANES - _N_GATES
    w = jnp.concatenate([w_in[:, a:b] for a, b in _EVEN_SRC] + [jnp.zeros((w_in.shape[0], pad), w_in.dtype)], axis=1)
    b = jnp.concatenate([b_in[a:b] for a, b in _EVEN_SRC] + [jnp.zeros((pad,), b_in.dtype)])
    return w.astype(BF16), b


def _even_plan(attn_dtype, q_scale, kv_kind=F32):
    kv = (kv_kind,) if attn_dtype == F32 else (kv_kind, attn_dtype)
    return ((0, 3 * W_M, (F32,), 1.0),
            (3 * W_M, 2 * W_M, (F32,), 1.0),
            (5 * W_M, W_F, (attn_dtype,), q_scale),
            (5 * W_M + W_F, W_F, kv, 1.0),
            (5 * W_M + 2 * W_F, W_F, kv, 1.0),
            (5 * W_M + 3 * W_F, W_F, (F32,), 1.0),
            (5 * W_M + 4 * W_F, LANES, (F32,), 1.0))


def _odd_plan(attn_dtype, q_scale, kv_kind=F32):
    kv = (kv_kind,) if attn_dtype == F32 else (kv_kind, attn_dtype)
    return ((0, W_D, (attn_dtype,), q_scale), (W_D, W_D, kv, 1.0), (2 * W_D, W_D, kv, 1.0),
            (3 * W_D, W_D, (F32,), 1.0))


def _tile(n, pref):
    t = min(n, pref)
    while n % t:
        t //= 2
    return t


def kernel(x_prompt, x_sample, cache_fox_k, cache_fox_v, cache_fox_logf, cache_diff_k, cache_diff_v, state_mlstm_c, state_mlstm_n, state_mlstm_m, state_mlstm_conv, page_table, norm_g, final_norm_g, w_in_even, b_in_even, conv_w, conv_b, mlstm_norm_g, w_out_even, w_in_odd, lambda_q1, lambda_k1, lambda_q2, lambda_k2, diff_norm_g, w_out_odd):
    B, T, D = x_prompt.shape
    Bs, Ts, _ = x_sample.shape
    depth = norm_g.shape[0]
    n_even, n_odd = (depth + 1) // 2, depth // 2
    n_pool, P = cache_fox_k.shape[1], cache_fox_k.shape[2]
    n_pages = page_table.shape[1]
    Mp, Ms = B * T, Bs * Ts
    tm_p, tm_s = _tile(Mp, 256), _tile(Ms, 256)
    tq = _tile(T, 512)
    q_scale = LOG2E * DH_F ** -0.5
    r_fox, r_diff = _tile(n_pages, 16), _tile(n_pages, 8)
    fox_kT_cache = jnp.transpose(cache_fox_k, (0, 1, 3, 4, 2))
    fox_vT_cache = jnp.transpose(cache_fox_v, (0, 1, 3, 4, 2))
    fox_lfT_cache = jnp.transpose(cache_fox_logf, (0, 1, 3, 2))

    hp = x_prompt.reshape(Mp, D)
    hs = x_sample.reshape(Ms, D)
    names = ("fox_k", "fox_v", "fox_logf", "diff_k", "diff_v", "mlstm_c", "mlstm_n", "mlstm_m", "mlstm_conv")
    outs_p = {nm: [] for nm in names}
    outs_s = {nm: [] for nm in names}

    for l in range(depth):
        i = l // 2
        last = l == depth - 1
        fg = final_norm_g if last else None
        if l % 2 == 0:
            w, b = _even_weights(w_in_even[i], b_in_even[i])
            wo = w_out_even[i].astype(BF16)
            wo_m, wo_f = wo[:W_M], wo[W_M:]
            qkv, oz, qf, kT_p, kfb, vT_p, vfb, zf, gt = _proj(
                hp, norm_g[l], w, b, _even_plan(BF16, q_scale, TIME_MINOR), tm_p,
                (i, n_even, T, None if i == 0 else (kT_p, vT_p)))
            hm, c_p, n_p, m_p, cv_p = _mlstm(qkv.reshape(B, T, -1), oz.reshape(B, T, -1), gt.reshape(B, T, -1),
                                             conv_w[i], conv_b[i], mlstm_norm_g[i], None, BF16)
            lff, ft = _fcum(gt.reshape(B, T, LANES))
            hf = _attn("fox", qf.reshape(B, T, W_F), kfb.reshape(B, T, W_F), vfb.reshape(B, T, W_F),
                       zf.reshape(B, T, W_F), (_fox_bias_layout(ft, tq),), 0.0, tq)
            hp = _outproj(hp, [hm.reshape(Mp, W_M), hf.reshape(Mp, W_F)], [wo_m, wo_f], fg, tm_p)
            outs_p["fox_logf"].append(lff[:, :, 2 * H_M:2 * H_M + H_F])
            outs_p["mlstm_c"].append(c_p)
            outs_p["mlstm_n"].append(n_p)
            outs_p["mlstm_m"].append(m_p[:, :, 0])
            outs_p["mlstm_conv"].append(cv_p)
            qkv, oz, qf, kf, vf, zf, gt = _proj(hs, norm_g[l], w, b, _even_plan(F32, 1.0), tm_s)
            state = (state_mlstm_c[i], state_mlstm_n[i],
                     jnp.broadcast_to(state_mlstm_m[i][:, :, None], (Bs, H_M, LANES)), state_mlstm_conv[i])
            hm, c_s, n_s, m_s, cv_s = _mlstm(qkv.reshape(Bs, Ts, -1), oz.reshape(Bs, Ts, -1), gt.reshape(Bs, Ts, -1),
                                             conv_w[i], conv_b[i], mlstm_norm_g[i], state, F32)
            hf, lffn = _decode_fox(page_table, fox_kT_cache, fox_vT_cache, fox_lfT_cache, i, qf.reshape(Bs, Ts, W_F),
                                   kf.reshape(Bs, Ts, W_F), vf.reshape(Bs, Ts, W_F), zf.reshape(Bs, Ts, W_F),
                                   gt.reshape(Bs, Ts, LANES), r_fox)
            hs = _outproj(hs, [hm.reshape(Ms, W_M), hf.reshape(Ms, W_F)], [wo_m, wo_f], fg, tm_s)
            outs_s["fox_k"].append(kf.reshape(Bs, Ts, H_F, DH_F))
            outs_s["fox_v"].append(vf.reshape(Bs, Ts, H_F, DH_F))
            outs_s["fox_logf"].append(lffn[:, :, 2 * H_M:2 * H_M + H_F])
            outs_s["mlstm_c"].append(c_s)
            outs_s["mlstm_n"].append(n_s)
            outs_s["mlstm_m"].append(m_s[:, :, 0])
            outs_s["mlstm_conv"].append(cv_s)
        else:
            lam_init = 0.8 - 0.6 * float(np.exp(-0.3 * l))
            w = w_in_odd[i].astype(BF16)
            wo = w_out_odd[i].astype(BF16)
            lam_p = (lambda_q1[i], lambda_k1[i], lambda_q2[i], lambda_k2[i], diff_norm_g[i])
            q, k_p, kb, v_p, vb, z = _proj(hp, norm_g[l], w, None, _odd_plan(BF16, q_scale, HEAD_MAJOR), tm_p,
                                           (i, n_odd, T, None if i == 0 else (k_p, v_p)))
            o = _attn("diff", q.reshape(B, T, W_D), kb.reshape(B, T, W_D), vb.reshape(B, T, W_D),
                      z.reshape(B, T, W_D), lam_p, lam_init, tq)
            hp = _outproj(hp, [o.reshape(Mp, W_D)], [wo], fg, tm_p)
            q, k, v, z = _proj(hs, norm_g[l], w, None, _odd_plan(F32, 1.0), tm_s)
            o = _decode_diff(page_table, cache_diff_k, cache_diff_v, i, q.reshape(Bs, Ts, W_D),
                             k.reshape(Bs, Ts * H_D, DV_D), v.reshape(Bs, Ts * H_D, DV_D), z.reshape(Bs, Ts, W_D),
                             lam_p, lam_init, r_diff)
            hs = _outproj(hs, [o.reshape(Ms, W_D)], [wo], fg, tm_s)
            outs_s["diff_k"].append(k.reshape(Bs, Ts, H_D, 2 * DQ_D))
            outs_s["diff_v"].append(v.reshape(Bs, Ts, H_D, DV_D))

    stacked_p = {
        "fox_k": kT_p.reshape(n_even, B, H_F, DH_F, T).transpose(0, 1, 4, 2, 3),
        "fox_v": vT_p.reshape(n_even, B, H_F, DH_F, T).transpose(0, 1, 4, 2, 3),
        "diff_k": k_p.reshape(n_odd, B, T, H_D, 2 * DQ_D),
        "diff_v": v_p.reshape(n_odd, B, T, H_D, DV_D),
    }
    return (hp.reshape(B, T, D), hs.reshape(Bs, Ts, D),
            *[stacked_p[nm] if nm in stacked_p else jnp.stack(outs_p[nm]) for nm in names],
            *[jnp.stack(outs_s[nm]) for nm in names])
```

```python
import functools

import numpy as np
import jax
import jax.numpy as jnp
from jax import lax
from jax.experimental import pallas as pl
from jax.experimental.pallas import tpu as pltpu

F32 = jnp.float32
BF16 = jnp.bfloat16
NEG_INF = float("-inf")

EPS = 1e-6
LOG2E = 1.4426950408889634
H_M, DH_M = 4, 128
H_F, DH_F = 8, 64
H_D, DV_D, DQ_D = 8, 128, 64
CONV_W = 4
M_CHUNK = 128
LANES = 128
SUBLANES = 8
VMEM_LIMIT = 52 * 1024 * 1024

W_M = H_M * DH_M
W_F = H_F * DH_F
W_D = H_D * DV_D


def _cparams(sem):
    return pltpu.CompilerParams(dimension_semantics=sem, vmem_limit_bytes=VMEM_LIMIT)


def _sigmoid(x):
    return 1.0 / (1.0 + jnp.exp(-x))


def _silu(x):
    return x * _sigmoid(x)


def _log_sigmoid(x):
    return jnp.minimum(x, 0.0) - jnp.log(1.0 + jnp.exp(-jnp.abs(x)))


def _dot_nt(a, b):
    return lax.dot_general(a, b, (((1,), (1,)), ((), ())), preferred_element_type=F32)


TIME_MINOR = "time_minor"
HEAD_MAJOR = "head_major"


def _proj_body(plan, has_bias, n_prev, x_ref, g_ref, w_ref, *rest):
    if has_bias:
        b_ref, *rest = rest
    out_refs = rest[n_prev:]
    x = x_ref[...]
    xn = x * lax.rsqrt(jnp.mean(x * x, axis=-1, keepdims=True) + EPS) * g_ref[...]
    xn = xn.astype(BF16)
    k = 0
    for c0, width, kinds, scale in plan:
        y = jnp.dot(xn, w_ref[:, c0:c0 + width], preferred_element_type=F32)
        if has_bias:
            y = y + b_ref[:, c0:c0 + width]
        if scale != 1.0:
            y = y * scale
        for kind in kinds:
            if kind == TIME_MINOR:
                out_refs[k][...] = y.T
            elif kind == HEAD_MAJOR:
                out_refs[k][...] = y.reshape(out_refs[k].shape)
            else:
                out_refs[k][...] = y.astype(kind)
            k += 1


def _proj(x, g, w, b, plan, tm, stack=None):
    M, D = x.shape
    N = w.shape[1]
    has_bias = b is not None
    layer, n_layers, T, prev = stack if stack is not None else (0, 1, M, None)
    out_shape, out_specs, stacked = [], [], []
    for _, width, kinds, _ in plan:
        for kind in kinds:
            if kind == TIME_MINOR:
                nt = T // tm
                out_shape.append(jax.ShapeDtypeStruct((n_layers, M // T, width, T), F32))
                out_specs.append(pl.BlockSpec((None, None, width, tm), lambda i: (layer, i // nt, 0, i % nt)))
                stacked.append(len(out_shape) - 1)
            elif kind == HEAD_MAJOR:
                out_shape.append(jax.ShapeDtypeStruct((n_layers, M, H_D, width // H_D), F32))
                out_specs.append(pl.BlockSpec((None, tm, H_D, width // H_D), lambda i: (layer, i, 0, 0)))
                stacked.append(len(out_shape) - 1)
            else:
                out_shape.append(jax.ShapeDtypeStruct((M, width), kind))
                out_specs.append(pl.BlockSpec((tm, width), lambda i: (i, 0)))
    in_specs = [pl.BlockSpec((tm, D), lambda i: (i, 0)),
                pl.BlockSpec((1, D), lambda i: (0, 0)),
                pl.BlockSpec((D, N), lambda i: (0, 0))]
    args = [x, g.reshape(1, D), w]
    if has_bias:
        in_specs.append(pl.BlockSpec((1, N), lambda i: (0, 0)))
        args.append(b.reshape(1, N))
    aliases = {}
    if prev is not None:
        for a, o in zip(prev, stacked):
            aliases[len(args)] = o
            in_specs.append(pl.BlockSpec(memory_space=pl.ANY))
            args.append(a)
    return pl.pallas_call(
        functools.partial(_proj_body, plan, has_bias, len(aliases)),
        grid=(M // tm,), in_specs=in_specs, out_specs=out_specs, out_shape=out_shape,
        input_output_aliases=aliases, compiler_params=_cparams(("arbitrary",)), name="proj",
    )(*args)


def _mlstm_body(Lv, has_state, qkv_ref, oz_ref, gt_ref, cw_ref, cb_ref, mg_ref, *rest):
    L = M_CHUNK
    if has_state:
        c0_ref, n0_ref, m0_ref, cv0_ref, *rest = rest
    h_ref, cN_ref, nN_ref, mN_ref, cvN_ref, ubuf, *pads = rest
    c = pl.program_id(1)

    @pl.when(c == 0)
    def _():
        ubuf[...] = jnp.zeros(ubuf.shape, F32)
        if has_state:
            ubuf[SUBLANES - (CONV_W - 1):SUBLANES, :] = cv0_ref[...]
            cN_ref[...] = c0_ref[...]
            nN_ref[...] = n0_ref[...]
            mN_ref[...] = m0_ref[...]
        else:
            cN_ref[...] = jnp.zeros(cN_ref.shape, F32)
            nN_ref[...] = jnp.zeros(nN_ref.shape, F32)
            mN_ref[...] = jnp.zeros(mN_ref.shape, F32)

    ubuf[SUBLANES:SUBLANES + Lv, :] = qkv_ref[:, 0:2 * W_M]
    acc = None
    for j in range(CONV_W):
        r0 = SUBLANES - (CONV_W - 1) + j
        term = ubuf[r0:r0 + L, :] * cw_ref[j:j + 1, :]
        acc = term if acc is None else acc + term
    y = acc + cb_ref[...]
    tail = ubuf[SUBLANES + Lv - (CONV_W - 1):SUBLANES + Lv, :]
    ubuf[SUBLANES - (CONV_W - 1):SUBLANES, :] = tail
    cvN_ref[...] = tail
    qk = _silu(y)

    if Lv == L:
        G = gt_ref[...]
        V = qkv_ref[:, 2 * W_M:3 * W_M]
    else:
        gbuf, vbuf = pads
        gbuf[...] = jnp.zeros(gbuf.shape, F32)
        gbuf[0:Lv, :] = gt_ref[...]
        vbuf[...] = jnp.zeros(vbuf.shape, F32)
        vbuf[0:Lv, :] = qkv_ref[:, 2 * W_M:3 * W_M]
        G = gbuf[...]
        V = vbuf[...]
    LF = _log_sigmoid(G)
    GT = G.T
    LFT = LF.T
    r_i = lax.broadcasted_iota(jnp.int32, (L, L), 0)
    c_i = lax.broadcasted_iota(jnp.int32, (L, L), 1)
    tril = r_i >= c_i
    triu = r_i <= c_i
    rowv = lax.broadcasted_iota(jnp.int32, (L, 1), 0) < Lv
    colv = lax.broadcasted_iota(jnp.int32, (1, L), 1) < Lv

    for h in range(H_M):
        hs = slice(h * DH_M, (h + 1) * DH_M)
        q = qk[:, h * DH_M:(h + 1) * DH_M]
        k = qk[:, W_M + h * DH_M:W_M + (h + 1) * DH_M] * (DH_M ** -0.5)
        vb = V[:, hs].astype(BF16)
        ig_row = GT[h:h + 1, :]
        lf_row = LFT[H_M + h:H_M + h + 1, :]
        ig_col = G[:, h:h + 1]
        lf_col = LF[:, H_M + h:H_M + h + 1]
        if Lv < L:
            lf_row = jnp.where(colv, lf_row, 0.0)
            lf_col = jnp.where(rowv, lf_col, 0.0)
            ig_row = jnp.where(colv, ig_row, NEG_INF)
            ig_col = jnp.where(rowv, ig_col, NEG_INF)
        F_col = jnp.sum(jnp.where(tril, lf_row, 0.0), axis=1, keepdims=True)
        F_row = jnp.sum(jnp.where(triu, lf_col, 0.0), axis=0, keepdims=True)
        Dm = jnp.where(tril, F_col - F_row + ig_row, NEG_INF)
        m_prev = mN_ref[h:h + 1, 0:1]
        inter = F_col + m_prev
        m_t = jnp.maximum(jnp.max(Dm, axis=1, keepdims=True), inter)
        qb = q.astype(BF16)
        S = jnp.exp(Dm - m_t) * _dot_nt(qb, k.astype(BF16))
        w_inter = jnp.exp(inter - m_t)
        C = cN_ref[h]
        n_row = nN_ref[h:h + 1, :]
        num = (jnp.dot(S.astype(BF16), vb, preferred_element_type=F32)
               + w_inter * jnp.dot(qb, C.astype(BF16), preferred_element_type=F32))
        den = jnp.sum(S, axis=1, keepdims=True) + w_inter * jnp.sum(q * n_row, axis=1, keepdims=True)
        hh = num / jnp.maximum(jnp.abs(den), jnp.exp(-m_t))

        m_new = m_t[Lv - 1:Lv, :]
        F_last = F_col[Lv - 1:Lv, :]
        w_s = jnp.exp(F_last - F_col + ig_col - m_new)
        decay = jnp.exp(F_last + m_prev - m_new)
        kw = k * w_s
        cN_ref[h] = decay * C + jnp.dot(kw.T.astype(BF16), vb, preferred_element_type=F32)
        nN_ref[h:h + 1, :] = decay * n_row + jnp.sum(kw, axis=0, keepdims=True)
        mN_ref[h:h + 1, :] = jnp.broadcast_to(m_new, (1, LANES))

        hn = hh * lax.rsqrt(jnp.mean(hh * hh, axis=1, keepdims=True) + EPS) * mg_ref[:, hs]
        hn = hn[0:Lv, :]
        o = oz_ref[:, h * DH_M:(h + 1) * DH_M]
        z = oz_ref[:, W_M + h * DH_M:W_M + (h + 1) * DH_M]
        h_ref[:, hs] = (hn * _sigmoid(o) * _silu(z)).astype(h_ref.dtype)


def _mlstm(qkv, oz, gt, conv_w, conv_b, mnorm_g, state, out_dtype):
    B, T, _ = qkv.shape
    Lv = min(T, M_CHUNK)
    nc = T // Lv
    has_state = state is not None
    row = lambda b, c: (b, c, 0)
    fix2 = lambda b, c: (0, 0)
    per_b3 = lambda b, c: (b, 0, 0)
    per_b4 = lambda b, c: (b, 0, 0, 0)
    in_specs = [pl.BlockSpec((None, Lv, 3 * W_M), row),
                pl.BlockSpec((None, Lv, 2 * W_M), row),
                pl.BlockSpec((None, Lv, LANES), row),
                pl.BlockSpec((CONV_W, 2 * W_M), fix2),
                pl.BlockSpec((1, 2 * W_M), fix2),
                pl.BlockSpec((1, W_M), fix2)]
    args = [qkv, oz, gt, conv_w, conv_b.reshape(1, -1), mnorm_g.reshape(1, -1)]
    state_specs = [pl.BlockSpec((None, H_M, DH_M, DH_M), per_b4),
                   pl.BlockSpec((None, H_M, DH_M), per_b3),
                   pl.BlockSpec((None, H_M, LANES), per_b3),
                   pl.BlockSpec((None, CONV_W - 1, 2 * W_M), per_b3)]
    if has_state:
        in_specs += state_specs
        args += list(state)
    out_shape = [jax.ShapeDtypeStruct((B, T, W_M), out_dtype),
                 jax.ShapeDtypeStruct((B, H_M, DH_M, DH_M), F32),
                 jax.ShapeDtypeStruct((B, H_M, DH_M), F32),
                 jax.ShapeDtypeStruct((B, H_M, LANES), F32),
                 jax.ShapeDtypeStruct((B, CONV_W - 1, 2 * W_M), F32)]
    out_specs = [pl.BlockSpec((None, Lv, W_M), row)] + state_specs
    scratch = [pltpu.VMEM((SUBLANES + M_CHUNK, 2 * W_M), F32)]
    if Lv < M_CHUNK:
        scratch += [pltpu.VMEM((M_CHUNK, LANES), F32), pltpu.VMEM((M_CHUNK, W_M), F32)]
    return pl.pallas_call(
        functools.partial(_mlstm_body, Lv, has_state),
        grid=(B, nc), in_specs=in_specs, out_specs=out_specs, out_shape=out_shape,
        scratch_shapes=scratch, compiler_params=_cparams(("arbitrary", "arbitrary")), name="mlstm",
    )(*args)


def _fcum_body(n_chunks, g_ref, lff_ref, ft_ref):
    L = LANES
    triu = lax.broadcasted_iota(jnp.int32, (L, L), 0) <= lax.broadcasted_iota(jnp.int32, (L, L), 1)

    def chunk(c, carry):
        r0 = pl.multiple_of(c * L, L)
        LF = _log_sigmoid(g_ref[pl.ds(r0, L), :])
        lff_ref[pl.ds(r0, L), :] = LF
        new = []
        for h in range(H_F):
            col = LF[:, H_F + h:H_F + h + 1]
            fr = jnp.sum(jnp.where(triu, col, 0.0), axis=0, keepdims=True) + carry[h]
            ft_ref[c, h:h + 1, :] = fr * LOG2E
            new.append(fr[:, L - 1:L])
        return tuple(new)

    lax.fori_loop(0, n_chunks, chunk, tuple(jnp.zeros((1, 1), F32) for _ in range(H_F)))


def _fcum(gt):
    B, T, _ = gt.shape
    nc = T // LANES
    return pl.pallas_call(
        functools.partial(_fcum_body, nc),
        grid=(B,),
        in_specs=[pl.BlockSpec((None, T, LANES), lambda b: (b, 0, 0))],
        out_specs=[pl.BlockSpec((None, T, LANES), lambda b: (b, 0, 0)),
                   pl.BlockSpec((None, nc, H_F, LANES), lambda b: (b, 0, 0, 0))],
        out_shape=[jax.ShapeDtypeStruct((B, T, LANES), F32), jax.ShapeDtypeStruct((B, nc, H_F, LANES), F32)],
        compiler_params=_cparams(("arbitrary",)), name="fcum",
    )(gt)


def _fox_bias_layout(ft, tq):
    B, nc, _, L = ft.shape
    per = tq // L
    x = ft.reshape(B, nc // per, per, H_F // 2, 2, L)
    return x.transpose(0, 3, 1, 4, 2, 5).reshape(B, H_F // 2, nc // per, 2, tq)


def _lambda(lq1_ref, lk1_ref, lq2_ref, lk2_ref, lam_init):
    return (jnp.exp(jnp.sum(lq1_ref[...] * lk1_ref[...], axis=1, keepdims=True))
            - jnp.exp(jnp.sum(lq2_ref[...] * lk2_ref[...], axis=1, keepdims=True)) + lam_init)


def _attn_body(mode, tq, lam_init, q_ref, k_ref, v_ref, z_ref, *rest):
    if mode == "fox":
        ft_ref, o_ref, q2, s_scr, p_scr, m_s, acc = rest
    else:
        lq1, lk1, lq2, lk2, dg_ref, o_ref, q2, s_scr, p_scr, m_s, acc = rest
    qi = pl.program_id(2)
    rows = 2 * tq
    sb = min(tq, 64)
    lane_row = lax.broadcasted_iota(jnp.int32, (1, LANES), 1)
    q = q_ref[...]
    q2[0:tq, :] = q * jnp.where(lane_row < DH_F, 1.0, 0.0).astype(BF16)
    q2[tq:rows, :] = q * jnp.where(lane_row >= DH_F, 1.0, 0.0).astype(BF16)
    m_s[...] = jnp.full(m_s.shape, NEG_INF, F32)

    @pl.when((pl.program_id(0) == 0) & (pl.program_id(1) == 0) & (qi == 0))
    def _():
        acc[...] = jnp.zeros(acc.shape, F32)
        p_scr[...] = jnp.zeros(p_scr.shape, BF16)

    ones = jnp.ones((tq, LANES), BF16)

    def scores(j):
        k0 = pl.multiple_of(j * tq, tq)
        k = k_ref[pl.ds(k0, tq), :]
        return jnp.concatenate([_dot_nt(q2[0:tq, :], k), _dot_nt(q2[tq:rows, :], k)], axis=0)

    def pv(j):
        k0 = pl.multiple_of(j * tq, tq)
        vext = jnp.concatenate([v_ref[pl.ds(k0, tq), :], ones], axis=1)
        return jnp.dot(p_scr[...], vext, preferred_element_type=F32)

    def softmax_rows(j, pend, masked, keep):
        outs = []
        for r0 in range(0, rows, sb):
            s = s_scr[r0:r0 + sb, :]
            if mode == "fox":
                a = 0 if r0 < tq else 1
                s = s - ft_ref[j, a:a + 1, :]
            if masked:
                r_i = lax.broadcasted_iota(jnp.int32, (sb, tq), 0) + (r0 & (tq - 1))
                c_i = lax.broadcasted_iota(jnp.int32, (sb, tq), 1)
                s = jnp.where(r_i >= c_i, s, NEG_INF)
            m_old = m_s[r0:r0 + sb, :]
            m_new = jnp.maximum(m_old, jnp.max(s, axis=1, keepdims=True))
            p = jnp.exp2(s - jnp.concatenate([m_new] * (tq // LANES), axis=1)).astype(BF16)
            alpha = jnp.exp2(m_old - m_new)
            m_s[r0:r0 + sb, :] = m_new
            a_new = jnp.concatenate([alpha, alpha], axis=1) * (acc[r0:r0 + sb, :] + pend[r0:r0 + sb, :])
            if keep:
                acc[r0:r0 + sb, :] = a_new
            else:
                outs.append(a_new)
            p_scr[r0:r0 + sb, :] = p
        return outs

    def body(j, c):
        pend = pv(jnp.maximum(j - 1, 0))
        s_scr[...] = scores(j)
        softmax_rows(j, pend, False, True)
        return c

    lax.fori_loop(0, qi, body, 0)
    s_scr[...] = scores(qi)
    pend = pv(jnp.maximum(qi - 1, 0))
    a = jnp.concatenate(softmax_rows(qi, pend, True, False), axis=0)
    a = a + pv(qi)
    o0 = a[0:tq, 0:LANES] / a[0:tq, LANES:2 * LANES]
    o1 = a[tq:rows, 0:LANES] / a[tq:rows, LANES:2 * LANES]
    z = z_ref[...]
    if mode == "fox":
        lane = lax.broadcasted_iota(jnp.int32, (tq, LANES), 1)
        o = jnp.where(lane < DH_F, o0, o1)
        o_ref[...] = (o * _silu(z)).astype(o_ref.dtype)
    else:
        lam = _lambda(lq1, lk1, lq2, lk2, lam_init)
        o = o0 - lam * o1
        o = o * lax.rsqrt(jnp.mean(o * o, axis=1, keepdims=True) + EPS) * dg_ref[...]
        o_ref[...] = (o * (1.0 - lam_init) * _silu(z)).astype(o_ref.dtype)


def _attn(mode, q, k, v, z, extra, lam_init, tq):
    B, T, C = q.shape
    G = C // LANES
    scratch = [pltpu.VMEM((2 * tq, LANES), BF16), pltpu.VMEM((2 * tq, tq), F32), pltpu.VMEM((2 * tq, tq), BF16),
               pltpu.VMEM((2 * tq, LANES), F32), pltpu.VMEM((2 * tq, 2 * LANES), F32)]
    qspec = pl.BlockSpec((None, tq, LANES), lambda b, g, i: (b, i, g))
    kspec = pl.BlockSpec((None, T, LANES), lambda b, g, i: (b, 0, g))
    in_specs = [qspec, kspec, kspec, qspec]
    if mode == "fox":
        (ft,) = extra
        in_specs.append(pl.BlockSpec((None, None, T // tq, 2, tq), lambda b, g, i: (b, g, 0, 0, 0)))
        args = [q, k, v, z, ft]
    else:
        lq1, lk1, lq2, lk2, dg = extra
        in_specs += [pl.BlockSpec((1, DQ_D), lambda b, g, i: (0, 0))] * 4
        in_specs.append(pl.BlockSpec((1, DV_D), lambda b, g, i: (0, 0)))
        args = [q, k, v, z] + [a.reshape(1, -1) for a in (lq1, lk1, lq2, lk2, dg)]
    return pl.pallas_call(
        functools.partial(_attn_body, mode, tq, lam_init),
        grid=(B, G, T // tq), in_specs=in_specs, out_specs=qspec,
        out_shape=jax.ShapeDtypeStruct((B, T, C), BF16), scratch_shapes=scratch,
        compiler_params=_cparams(("arbitrary", "arbitrary", "arbitrary")), name="attn_" + mode,
    )(*args)


def _outproj_body(n_in, final, x_ref, *rest):
    a_refs, w_refs = rest[:n_in], rest[n_in:2 * n_in]
    rest = rest[2 * n_in:]
    acc = None
    for a_ref, w_ref in zip(a_refs, w_refs):
        t = jnp.dot(a_ref[...].astype(BF16), w_ref[...], preferred_element_type=F32)
        acc = t if acc is None else acc + t
    y = x_ref[...] + acc
    if final:
        g_ref, o_ref = rest
        y = y * lax.rsqrt(jnp.mean(y * y, axis=-1, keepdims=True) + EPS) * g_ref[...]
    else:
        (o_ref,) = rest
    o_ref[...] = y


def _outproj(x, acts, ws, final_g, tm):
    M, D = x.shape
    n_in = len(acts)
    final = final_g is not None
    in_specs = [pl.BlockSpec((tm, D), lambda i: (i, 0))]
    in_specs += [pl.BlockSpec((tm, a.shape[1]), lambda i: (i, 0)) for a in acts]
    in_specs += [pl.BlockSpec(w.shape, lambda i: (0, 0)) for w in ws]
    args = [x, *acts, *ws]
    if final:
        in_specs.append(pl.BlockSpec((1, D), lambda i: (0, 0)))
        args.append(final_g.reshape(1, D))
    return pl.pallas_call(
        functools.partial(_outproj_body, n_in, final),
        grid=(M // tm,), in_specs=in_specs, out_specs=pl.BlockSpec((tm, D), lambda i: (i, 0)),
        out_shape=jax.ShapeDtypeStruct((M, D), F32),
        compiler_params=_cparams(("arbitrary",)), name="outproj",
    )(*args)


def _softmax_step(s, m_s, l_s):
    m_old = m_s[...]
    m_new = jnp.maximum(m_old, jnp.max(s, axis=1, keepdims=True))
    pr = jnp.exp(s - m_new)
    alpha = jnp.exp(m_old - m_new)
    l_s[...] = alpha * l_s[...] + jnp.sum(pr, axis=1, keepdims=True)
    m_s[...] = m_new
    return pr.astype(BF16), alpha


def _decode_fox_body(R, Ts, pt_ref, q_ref, kn_ref, vn_ref, z_ref, g_ref, *refs):
    lf_refs, k_refs, v_refs = refs[:R], refs[R:2 * R], refs[2 * R:3 * R]
    o_ref, lffn_ref, qm, m_s, l_s, acc, carry, kpad, vpad, gpad = refs[3 * R:]
    P = LANES
    rows = H_F * SUBLANES
    p = pl.program_id(1)

    @pl.when(p == 0)
    def _():
        lane = lax.broadcasted_iota(jnp.int32, (Ts, W_F), 1)
        q = q_ref[...] * (DH_F ** -0.5)
        qm[...] = jnp.zeros(qm.shape, F32)
        for h in range(H_F):
            qm[h * SUBLANES:h * SUBLANES + Ts, :] = jnp.where((lane >= h * DH_F) & (lane < (h + 1) * DH_F), q, 0.0)
        m_s[...] = jnp.full(m_s.shape, NEG_INF, F32)
        l_s[...] = jnp.zeros(l_s.shape, F32)
        acc[...] = jnp.zeros(acc.shape, F32)
        carry[...] = jnp.zeros(carry.shape, F32)

    qb = qm[...].astype(BF16)

    def bias_rows(f):
        return jnp.concatenate([jnp.broadcast_to(f[h:h + 1, :], (SUBLANES, P)) for h in range(H_F)], axis=0)

    lf = jnp.concatenate([lf_refs[r][...] for r in range(R)], axis=0)
    tri = (lax.broadcasted_iota(jnp.int32, (P, P), 0) <= lax.broadcasted_iota(jnp.int32, (P, P), 1))
    tri = jnp.where(tri, 1.0, 0.0).astype(BF16)
    hi = lf.astype(BF16)
    rem = lf - hi.astype(F32)
    mid = rem.astype(BF16)
    lo = (rem - mid.astype(F32)).astype(BF16)
    f = (jnp.dot(hi, tri, preferred_element_type=F32) + jnp.dot(mid, tri, preferred_element_type=F32)
         + jnp.dot(lo, tri, preferred_element_type=F32))
    totals = [jnp.broadcast_to(f[r * H_F:(r + 1) * H_F, P - 1:P], (H_F, P)) for r in range(R)]
    run = carry[...]
    s_parts = []
    for r in range(R):
        kT = k_refs[r][...].reshape(W_F, P).astype(BF16)
        s_parts.append(jnp.dot(qb, kT, preferred_element_type=F32) - bias_rows(f[r * H_F:(r + 1) * H_F, :] + run))
        run = run + totals[r]
    carry[...] = run
    pr, alpha = _softmax_step(jnp.concatenate(s_parts, axis=1), m_s, l_s)
    pv = None
    for r in range(R):
        vT = v_refs[r][...].reshape(W_F, P).astype(BF16)
        t = _dot_nt(pr[:, r * P:(r + 1) * P], vT)
        pv = t if pv is None else pv + t
    acc[...] = alpha * acc[...] + pv

    @pl.when(p == pl.num_programs(1) - 1)
    def _():
        lf = _log_sigmoid(g_ref[...])
        lffn_ref[...] = lf
        gpad[...] = jnp.zeros(gpad.shape, F32)
        gpad[0:Ts, :] = lf
        lfp = gpad[...]
        triu = lax.broadcasted_iota(jnp.int32, (P, P), 0) <= lax.broadcasted_iota(jnp.int32, (P, P), 1)
        fn = [jnp.sum(jnp.where(triu, lfp[:, 2 * H_M + h:2 * H_M + h + 1], 0.0), axis=0, keepdims=True)
              + carry[h:h + 1, :] for h in range(H_F)]
        bias = jnp.concatenate([jnp.broadcast_to(f, (SUBLANES, P)) for f in fn], axis=0)
        kpad[...] = jnp.zeros(kpad.shape, F32)
        kpad[0:Ts, :] = kn_ref[...]
        vpad[...] = jnp.zeros(vpad.shape, F32)
        vpad[0:Ts, :] = vn_ref[...]
        t_i = lax.broadcasted_iota(jnp.int32, (rows, P), 0) & (SUBLANES - 1)
        j_i = lax.broadcasted_iota(jnp.int32, (rows, P), 1)
        s = _dot_nt(qb, kpad[...].astype(BF16)) - bias
        s = jnp.where((j_i <= t_i) & (j_i < Ts), s, NEG_INF)
        pr, alpha = _softmax_step(s, m_s, l_s)
        o_full = (alpha * acc[...] + jnp.dot(pr, vpad[...].astype(BF16), preferred_element_type=F32)) / l_s[...]
        z = z_ref[...]
        lane = lax.broadcasted_iota(jnp.int32, (SUBLANES, LANES), 1)
        for j in range(W_F // LANES):
            cs = slice(j * LANES, (j + 1) * LANES)
            o = jnp.where(lane < DH_F, o_full[(2 * j) * SUBLANES:(2 * j + 1) * SUBLANES, cs],
                          o_full[(2 * j + 1) * SUBLANES:(2 * j + 2) * SUBLANES, cs])
            o_ref[:, cs] = o[0:Ts, :] * _silu(z[:, cs])


def _decode_fox(page_table, kT_cache, vT_cache, lfT_cache, layer, q, k_new, v_new, z, gt, R):
    Bs, n_pages = page_table.shape
    _, Ts, C = q.shape
    P = kT_cache.shape[-1]
    rows = H_F * SUBLANES
    tok = pl.BlockSpec((None, Ts, C), lambda b, p, pt: (b, 0, 0))
    gspec = pl.BlockSpec((None, Ts, LANES), lambda b, p, pt: (b, 0, 0))
    lf_pages = [pl.BlockSpec((None, None, H_F, P), lambda b, p, pt, r=r: (layer, pt[b, p * R + r], 0, 0))
                for r in range(R)]
    kv_pages = [pl.BlockSpec((None, None, H_F, DH_F, P), lambda b, p, pt, r=r: (layer, pt[b, p * R + r], 0, 0, 0))
                for r in range(R)]
    grid_spec = pltpu.PrefetchScalarGridSpec(
        num_scalar_prefetch=1, grid=(Bs, n_pages // R),
        in_specs=[tok, tok, tok, tok, gspec] + lf_pages + kv_pages + kv_pages, out_specs=[tok, gspec],
        scratch_shapes=[pltpu.VMEM((rows, C), F32), pltpu.VMEM((rows, 1), F32), pltpu.VMEM((rows, 1), F32),
                        pltpu.VMEM((rows, C), F32), pltpu.VMEM((H_F, P), F32),
                        pltpu.VMEM((P, C), F32), pltpu.VMEM((P, C), F32), pltpu.VMEM((P, LANES), F32)])
    return pl.pallas_call(
        functools.partial(_decode_fox_body, R, Ts), grid_spec=grid_spec,
        out_shape=[jax.ShapeDtypeStruct((Bs, Ts, C), F32), jax.ShapeDtypeStruct((Bs, Ts, LANES), F32)],
        compiler_params=_cparams(("arbitrary", "arbitrary")), name="decode_fox",
    )(page_table, q, k_new, v_new, z, gt, *([lfT_cache] * R), *([kT_cache] * R), *([vT_cache] * R))


def _query_rows(Ts):
    return SUBLANES // 2 if Ts <= SUBLANES // 2 else SUBLANES


def _decode_diff_body(R, Ts, lam_init, pt_ref, q_ref, kn_ref, vn_ref, z_ref, lq1, lk1, lq2, lk2, dg_ref, *refs):
    k_refs, v_refs = refs[:R], refs[R:2 * R]
    o_ref, qm, m_s, l_s, acc, maskb, kpad, vpad = refs[2 * R:]
    P = k_refs[0].shape[0]
    tp = _query_rows(Ts)
    rows = 2 * H_D * tp
    grp = 2 * tp
    p = pl.program_id(1)

    @pl.when(p == 0)
    def _():
        lane = lax.broadcasted_iota(jnp.int32, (Ts, DV_D), 1)
        q = q_ref[...] * (DQ_D ** -0.5)
        qm[...] = jnp.zeros(qm.shape, F32)
        for h in range(H_D):
            qh = q[:, h * DV_D:(h + 1) * DV_D]
            qm[h * grp:h * grp + Ts, :] = jnp.where(lane < DQ_D, qh, 0.0)
            qm[h * grp + tp:h * grp + tp + Ts, :] = jnp.where(lane >= DQ_D, qh, 0.0)
        row_head = lax.shift_right_logical(lax.broadcasted_iota(jnp.int32, (rows, P * H_D), 0), grp.bit_length() - 1)
        col_head = lax.broadcasted_iota(jnp.int32, (rows, P * H_D), 1) & (H_D - 1)
        maskb[...] = jnp.where(row_head == col_head, 0.0, NEG_INF)
        m_s[...] = jnp.full(m_s.shape, NEG_INF, F32)
        l_s[...] = jnp.zeros(l_s.shape, F32)
        acc[...] = jnp.zeros(acc.shape, F32)

    qb = qm[...].astype(BF16)
    mb = maskb[...]
    s_parts = [_dot_nt(qb, k_refs[r][...].reshape(P * H_D, DV_D).astype(BF16)) + mb for r in range(R)]
    pr, alpha = _softmax_step(jnp.concatenate(s_parts, axis=1), m_s, l_s)
    pv = None
    for r in range(R):
        vv = v_refs[r][...].reshape(P * H_D, DV_D).astype(BF16)
        t = jnp.dot(pr[:, r * P * H_D:(r + 1) * P * H_D], vv, preferred_element_type=F32)
        pv = t if pv is None else pv + t
    acc[...] = alpha * acc[...] + pv

    @pl.when(p == pl.num_programs(1) - 1)
    def _():
        kpad[...] = jnp.zeros(kpad.shape, F32)
        kpad[0:Ts * H_D, :] = kn_ref[...]
        vpad[...] = jnp.zeros(vpad.shape, F32)
        vpad[0:Ts * H_D, :] = vn_ref[...]
        n_col = kpad.shape[0]
        r_i = lax.broadcasted_iota(jnp.int32, (rows, n_col), 0)
        j_i = lax.broadcasted_iota(jnp.int32, (rows, n_col), 1)
        key = lax.shift_right_logical(j_i, H_D.bit_length() - 1)
        row_head = lax.shift_right_logical(r_i, grp.bit_length() - 1)
        ok = ((j_i & (H_D - 1)) == row_head) & (key <= (r_i & (tp - 1))) & (key < Ts)
        s = jnp.where(ok, _dot_nt(qb, kpad[...].astype(BF16)), NEG_INF)
        pr, alpha = _softmax_step(s, m_s, l_s)
        o_full = (alpha * acc[...] + jnp.dot(pr, vpad[...].astype(BF16), preferred_element_type=F32)) / l_s[...]
        z = z_ref[...]
        lam = _lambda(lq1, lk1, lq2, lk2, lam_init)
        for h in range(H_D):
            cs = slice(h * DV_D, (h + 1) * DV_D)
            o = o_full[h * grp:h * grp + tp, :] - lam * o_full[h * grp + tp:(h + 1) * grp, :]
            o = o * lax.rsqrt(jnp.mean(o * o, axis=1, keepdims=True) + EPS) * dg_ref[...]
            o_ref[:, cs] = o[0:Ts, :] * (1.0 - lam_init) * _silu(z[:, cs])


def _decode_diff(page_table, k_cache, v_cache, layer, q, k_new, v_new, z, lam_params, lam_init, R):
    Bs, n_pages = page_table.shape
    _, Ts, C = q.shape
    P = k_cache.shape[2]
    rows = 2 * H_D * _query_rows(Ts)
    tok = pl.BlockSpec((None, Ts, C), lambda b, p, pt: (b, 0, 0))
    new = pl.BlockSpec((None, Ts * H_D, DV_D), lambda b, p, pt: (b, 0, 0))
    small = [pl.BlockSpec((1, DQ_D), lambda b, p, pt: (0, 0))] * 4 + [pl.BlockSpec((1, DV_D), lambda b, p, pt: (0, 0))]
    pages = [pl.BlockSpec((None, None, P, H_D, DV_D), lambda b, p, pt, r=r: (layer, pt[b, p * R + r], 0, 0, 0))
             for r in range(R)]
    grid_spec = pltpu.PrefetchScalarGridSpec(
        num_scalar_prefetch=1, grid=(Bs, n_pages // R),
        in_specs=[tok, new, new, tok] + small + pages + pages, out_specs=tok,
        scratch_shapes=[pltpu.VMEM((rows, DV_D), F32), pltpu.VMEM((rows, 1), F32), pltpu.VMEM((rows, 1), F32),
                        pltpu.VMEM((rows, DV_D), F32), pltpu.VMEM((rows, P * H_D), F32),
                        pltpu.VMEM((2 * SUBLANES * H_D, DV_D), F32), pltpu.VMEM((2 * SUBLANES * H_D, DV_D), F32)])
    return pl.pallas_call(
        functools.partial(_decode_diff_body, R, Ts, lam_init), grid_spec=grid_spec,
        out_shape=jax.ShapeDtypeStruct((Bs, Ts, C), F32),
        compiler_params=_cparams(("arbitrary", "arbitrary")), name="decode_diff",
    )(page_table, q, k_new, v_new, z, *[a.reshape(1, -1) for a in lam_params], *([k_cache] * R), *([v_cache] * R))


_EVEN_SRC = ((0, 5 * W_M), (5 * W_M + 2 * H_M, 5 * W_M + 2 * H_M + 4 * W_F),
             (5 * W_M, 5 * W_M + 2 * H_M), (5 * W_M + 2 * H_M + 4 * W_F, 5 * W_M + 2 * H_M + 4 * W_F + H_F))
_N_GATES = 2 * H_M + H_F


def _even_weights(w_in, b_in):
    pad = LANES - _N_GATES
    w = jnp.concatenate([w_in[:, a:b] for a, b in _EVEN_SRC] + [jnp.zeros((w_in.shape[0], pad), w_in.dtype)], axis=1)
    b = jnp.concatenate([b_in[a:b] for a, b in _EVEN_SRC] + [jnp.zeros((pad,), b_in.dtype)])
    return w.astype(BF16), b


def _even_plan(attn_dtype, q_scale, kv_kind=F32):
    kv = (kv_kind,) if attn_dtype == F32 else (kv_kind, attn_dtype)
    return ((0, 3 * W_M, (F32,), 1.0),
            (3 * W_M, 2 * W_M, (F32,), 1.0),
            (5 * W_M, W_F, (attn_dtype,), q_scale),
            (5 * W_M + W_F, W_F, kv, 1.0),
            (5 * W_M + 2 * W_F, W_F, kv, 1.0),
            (5 * W_M + 3 * W_F, W_F, (F32,), 1.0),
            (5 * W_M + 4 * W_F, LANES, (F32,), 1.0))


def _odd_plan(attn_dtype, q_scale, kv_kind=F32):
    kv = (kv_kind,) if attn_dtype == F32 else (kv_kind, attn_dtype)
    return ((0, W_D, (attn_dtype,), q_scale), (W_D, W_D, kv, 1.0), (2 * W_D, W_D, kv, 1.0),
            (3 * W_D, W_D, (F32,), 1.0))


def _tile(n, pref):
    t = min(n, pref)
    while n % t:
        t //= 2
    return t


def kernel(x_prompt, x_sample, cache_fox_k, cache_fox_v, cache_fox_logf, cache_diff_k, cache_diff_v, state_mlstm_c, state_mlstm_n, state_mlstm_m, state_mlstm_conv, page_table, norm_g, final_norm_g, w_in_even, b_in_even, conv_w, conv_b, mlstm_norm_g, w_out_even, w_in_odd, lambda_q1, lambda_k1, lambda_q2, lambda_k2, diff_norm_g, w_out_odd):
    B, T, D = x_prompt.shape
    Bs, Ts, _ = x_sample.shape
    depth = norm_g.shape[0]
    n_even, n_odd = (depth + 1) // 2, depth // 2
    n_pool, P = cache_fox_k.shape[1], cache_fox_k.shape[2]
    n_pages = page_table.shape[1]
    Mp, Ms = B * T, Bs * Ts
    tm_p, tm_s = _tile(Mp, 256), _tile(Ms, 256)
    tq = _tile(T, 512)
    q_scale = LOG2E * DH_F ** -0.5
    r_fox, r_diff = _tile(n_pages, 16), _tile(n_pages, 8)
    fox_kT_cache = jnp.transpose(cache_fox_k, (0, 1, 3, 4, 2))
    fox_vT_cache = jnp.transpose(cache_fox_v, (0, 1, 3, 4, 2))
    fox_lfT_cache = jnp.transpose(cache_fox_logf, (0, 1, 3, 2))

    hp = x_prompt.reshape(Mp, D)
    hs = x_sample.reshape(Ms, D)
    names = ("fox_k", "fox_v", "fox_logf", "diff_k", "diff_v", "mlstm_c", "mlstm_n", "mlstm_m", "mlstm_conv")
    outs_p = {nm: [] for nm in names}
    outs_s = {nm: [] for nm in names}

    for l in range(depth):
        i = l // 2
        last = l == depth - 1
        fg = final_norm_g if last else None
        if l % 2 == 0:
            w, b = _even_weights(w_in_even[i], b_in_even[i])
            wo = w_out_even[i].astype(BF16)
            wo_m, wo_f = wo[:W_M], wo[W_M:]
            qkv, oz, qf, kT_p, kfb, vT_p, vfb, zf, gt = _proj(
                hp, norm_g[l], w, b, _even_plan(BF16, q_scale, TIME_MINOR), tm_p,
                (i, n_even, T, None if i == 0 else (kT_p, vT_p)))
            hm, c_p, n_p, m_p, cv_p = _mlstm(qkv.reshape(B, T, -1), oz.reshape(B, T, -1), gt.reshape(B, T, -1),
                                             conv_w[i], conv_b[i], mlstm_norm_g[i], None, BF16)
            lff, ft = _fcum(gt.reshape(B, T, LANES))
            hf = _attn("fox", qf.reshape(B, T, W_F), kfb.reshape(B, T, W_F), vfb.reshape(B, T, W_F),
                       zf.reshape(B, T, W_F), (_fox_bias_layout(ft, tq),), 0.0, tq)
            hp = _outproj(hp, [hm.reshape(Mp, W_M), hf.reshape(Mp, W_F)], [wo_m, wo_f], fg, tm_p)
            outs_p["fox_logf"].append(lff[:, :, 2 * H_M:2 * H_M + H_F])
            outs_p["mlstm_c"].append(c_p)
            outs_p["mlstm_n"].append(n_p)
            outs_p["mlstm_m"].append(m_p[:, :, 0])
            outs_p["mlstm_conv"].append(cv_p)
            qkv, oz, qf, kf, vf, zf, gt = _proj(hs, norm_g[l], w, b, _even_plan(F32, 1.0), tm_s)
            state = (state_mlstm_c[i], state_mlstm_n[i],
                     jnp.broadcast_to(state_mlstm_m[i][:, :, None], (Bs, H_M, LANES)), state_mlstm_conv[i])
            hm, c_s, n_s, m_s, cv_s = _mlstm(qkv.reshape(Bs, Ts, -1), oz.reshape(Bs, Ts, -1), gt.reshape(Bs, Ts, -1),
                                             conv_w[i], conv_b[i], mlstm_norm_g[i], state, F32)
            hf, lffn = _decode_fox(page_table, fox_kT_cache, fox_vT_cache, fox_lfT_cache, i, qf.reshape(Bs, Ts, W_F),
                                   kf.reshape(Bs, Ts, W_F), vf.reshape(Bs, Ts, W_F), zf.reshape(Bs, Ts, W_F),
                                   gt.reshape(Bs, Ts, LANES), r_fox)
            hs = _outproj(hs, [hm.reshape(Ms, W_M), hf.reshape(Ms, W_F)], [wo_m, wo_f], fg, tm_s)
            outs_s["fox_k"].append(kf.reshape(Bs, Ts, H_F, DH_F))
            outs_s["fox_v"].append(vf.reshape(Bs, Ts, H_F, DH_F))
            outs_s["fox_logf"].append(lffn[:, :, 2 * H_M:2 * H_M + H_F])
            outs_s["mlstm_c"].append(c_s)
            outs_s["mlstm_n"].append(n_s)
            outs_s["mlstm_m"].append(m_s[:, :, 0])
            outs_s["mlstm_conv"].append(cv_s)
        else:
            lam_init = 0.8 - 0.6 * float(np.exp(-0.3 * l))
            w = w_in_odd[i].astype(BF16)
            wo = w_out_odd[i].astype(BF16)
            lam_p = (lambda_q1[i], lambda_k1[i], lambda_q2[i], lambda_k2[i], diff_norm_g[i])
            q, k_p, kb, v_p, vb, z = _proj(hp, norm_g[l], w, None, _odd_plan(BF16, q_scale, HEAD_MAJOR), tm_p,
                                           (i, n_odd, T, None if i == 0 else (k_p, v_p)))
            o = _attn("diff", q.reshape(B, T, W_D), kb.reshape(B, T, W_D), vb.reshape(B, T, W_D),
                      z.reshape(B, T, W_D), lam_p, lam_init, tq)
            hp = _outproj(hp, [o.reshape(Mp, W_D)], [wo], fg, tm_p)
            q, k, v, z = _proj(hs, norm_g[l], w, None, _odd_plan(F32, 1.0), tm_s)
            o = _decode_diff(page_table, cache_diff_k, cache_diff_v, i, q.reshape(Bs, Ts, W_D),
                             k.reshape(Bs, Ts * H_D, DV_D), v.reshape(Bs, Ts * H_D, DV_D), z.reshape(Bs, Ts, W_D),
                             lam_p, lam_init, r_diff)
            hs = _outproj(hs, [o.reshape(Ms, W_D)], [wo], fg, tm_s)
            outs_s["diff_k"].append(k.reshape(Bs, Ts, H_D, 2 * DQ_D))
            outs_s["diff_v"].append(v.reshape(Bs, Ts, H_D, DV_D))

    stacked_p = {
        "fox_k": kT_p.reshape(n_even, B, H_F, DH_F, T).transpose(0, 1, 4, 2, 3),
        "fox_v": vT_p.reshape(n_even, B, H_F, DH_F, T).transpose(0, 1, 4, 2, 3),
        "diff_k": k_p.reshape(n_odd, B, T, H_D, 2 * DQ_D),
        "diff_v": v_p.reshape(n_odd, B, T, H_D, DV_D),
    }
    return (hp.reshape(B, T, D), hs.reshape(Bs, Ts, D),
            *[stacked_p[nm] if nm in stacked_p else jnp.stack(outs_p[nm]) for nm in names],
            *[jnp.stack(outs_s[nm]) for nm in names])
```
